```python
import jax, jax.numpy as jnp
from jax import lax
import numpy as np

D_MODEL = 1024
BATCH = 4
SEQ = 8192
DEPTH = 1

GRID_W = 64
CTX_LEN = 256
CONF_WIDTH = D_MODEL
CONF_KERNEL = 31
LRU_WIDTH = 1280
LRU_BLOCKS = 10
LRU_BLOCK_W = LRU_WIDTH // LRU_BLOCKS
LRU_CONV = 4
LRU_PAD = (2, 1)
LRU_C = 8.0
FFN_HIDDEN = 2816
N_MOD = 9
EPS = 1e-6
COL_LRU_X = 2 * CONF_WIDTH
COL_LRU_G = COL_LRU_X + LRU_WIDTH
COL_GATE = COL_LRU_G + LRU_WIDTH
D_IN = COL_GATE + 2 * D_MODEL

kernel_name = "hybrid_conformer_rglru_prefix_block"


def rmsnorm(x, g):
    x32 = x.astype(jnp.float32)
    y = x32 * lax.rsqrt(jnp.mean(x32 * x32, axis=-1, keepdims=True) + EPS)
    return y.astype(x.dtype) * g


def layernorm(x, g, b):
    x32 = x.astype(jnp.float32)
    mu = jnp.mean(x32, axis=-1, keepdims=True)
    xc = x32 - mu
    y = xc * lax.rsqrt(jnp.mean(xc * xc, axis=-1, keepdims=True) + EPS)
    return y.astype(x.dtype) * g + b


def modulate(x, shift, scale):
    return x * (1 + scale) + shift


def depthwise_conv(x, w, b, pad):
    y = lax.conv_general_dilated(x, w[:, None, :], window_strides=(1,), padding=[pad],
                                 dimension_numbers=("NWC", "WIO", "NWC"),
                                 feature_group_count=x.shape[-1])
    return y + b


def swiglu(x, w_up, w_down):
    g, u = jnp.split(x @ w_up, 2, axis=-1)
    return (jax.nn.silu(g) * u) @ w_down


def grid_pos_embedding(seq_len, dim):
    rows = seq_len // GRID_W
    t = jnp.arange(rows * GRID_W)
    row = (t // GRID_W).astype(jnp.float32)
    col = (t % GRID_W).astype(jnp.float32)
    q = dim // 4
    omega = 1.0 / (10000.0 ** (jnp.arange(q, dtype=jnp.float32) / q))
    er = row[:, None] * omega
    ec = col[:, None] * omega
    return jnp.concatenate([jnp.sin(er), jnp.cos(er), jnp.sin(ec), jnp.cos(ec)], axis=-1)


def _combine(left, right):
    a1, b1 = left
    a2, b2 = right
    return a1 * a2, a2 * b1 + b2


def linear_scan(a, b, h0, reverse):
    first = -1 if reverse else 0
    b = b.at[:, first].add(a[:, first] * h0)
    _, h = lax.associative_scan(_combine, (a, b), axis=1, reverse=reverse)
    return h


def rglru_direction(xr, w_a, b_a, w_x, b_x, lam, h0, reverse):
    B, T, _ = xr.shape
    xb = xr.reshape(B, T, LRU_BLOCKS, LRU_BLOCK_W)
    r = jax.nn.sigmoid(jnp.einsum("bthi,hij->bthj", xb, w_a).reshape(B, T, LRU_WIDTH) + b_a)
    ig = jax.nn.sigmoid(jnp.einsum("bthi,hij->bthj", xb, w_x).reshape(B, T, LRU_WIDTH) + b_x)
    log_a = -LRU_C * r.astype(jnp.float32) * jax.nn.softplus(-lam.astype(jnp.float32))
    a = jnp.exp(log_a)
    bb = jnp.sqrt(-jnp.expm1(2.0 * log_a)) * (ig * xr).astype(jnp.float32)
    return linear_scan(a, bb, h0, reverse)


def rglru_scans(u_x, lp, h0f, h0b):
    xr = depthwise_conv(u_x, lp["w_lru_conv"], lp["b_lru_conv"], LRU_PAD)
    hf = rglru_direction(xr, lp["w_rec_gate"][0], lp["b_rec_gate"][0], lp["w_in_gate"][0],
                         lp["b_in_gate"][0], lp["lru_lambda"][0], h0f, False)
    hb = rglru_direction(xr, lp["w_rec_gate"][1], lp["b_rec_gate"][1], lp["w_in_gate"][1],
                         lp["b_in_gate"][1], lp["lru_lambda"][1], h0b, True)
    return xr, hf, hb


def conformer_branch(u_glu, lp):
    v, gt = jnp.split(u_glu, 2, axis=-1)
    u = v * jax.nn.sigmoid(gt)
    u = depthwise_conv(u, lp["w_dw"], lp["b_dw"], (CONF_KERNEL // 2, CONF_KERNEL // 2))
    u = jax.nn.silu(layernorm(u, lp["g_ln"], lp["b_ln"]))
    return u @ lp["w_conf_out"]


def mixer(h, lp, h0f, h0b):
    proj = h @ lp["w_in"] + lp["b_in"]
    y_conf = conformer_branch(proj[..., :COL_LRU_X], lp)
    xr, hf, hb = rglru_scans(proj[..., COL_LRU_X:COL_LRU_G], lp, h0f, h0b)
    y_lru = ((hf + hb).astype(xr.dtype) * jax.nn.gelu(proj[..., COL_LRU_G:COL_GATE])) @ lp["w_lru_out"]
    g_conf, g_lru = jnp.split(jax.nn.sigmoid(proj[..., COL_GATE:]), 2, axis=-1)
    y = (g_conf * y_conf + g_lru * y_lru) @ lp["w_out"]
    return y, hf[:, -1], hb[:, 0]


def context_lru_states(hc, lp):
    u_x = hc @ lp["w_in"][:, COL_LRU_X:COL_LRU_G] + lp["b_in"][COL_LRU_X:COL_LRU_G]
    h0 = jnp.zeros((hc.shape[0], LRU_WIDTH), jnp.float32)
    _, hf, hb = rglru_scans(u_x, lp, h0, h0)
    return hf[:, -1], hb[:, 0]


def layer(x, xc, c, c_ctx, lp, update_context):
    m = jnp.split((jax.nn.silu(c) @ lp["w_mod"] + lp["b_mod"])[:, None, :], N_MOD, axis=-1)
    mc = jnp.split((jax.nn.silu(c_ctx) @ lp["w_mod"] + lp["b_mod"])[None, None, :], N_MOD, axis=-1)
    x = x + 0.5 * m[2] * swiglu(modulate(rmsnorm(x, lp["g_n1"]), m[0], m[1]), lp["w_ffn1_up"], lp["w_ffn1_down"])
    xc = xc + 0.5 * mc[2] * swiglu(modulate(rmsnorm(xc, lp["g_n1"]), mc[0], mc[1]), lp["w_ffn1_up"], lp["w_ffn1_down"])
    hc = modulate(rmsnorm(xc, lp["g_n2"]), mc[3], mc[4])
    if update_context:
        h0 = jnp.zeros((hc.shape[0], LRU_WIDTH), jnp.float32)
        yc, hf_c, hb_c = mixer(hc, lp, h0, h0)
        xc = xc + mc[5] * yc
    else:
        hf_c, hb_c = context_lru_states(hc, lp)
    h = modulate(rmsnorm(x, lp["g_n2"]), m[3], m[4])
    y, _, _ = mixer(h, lp, hf_c, hb_c)
    x = x + m[5] * y
    x = x + 0.5 * m[8] * swiglu(modulate(rmsnorm(x, lp["g_n3"]), m[6], m[7]), lp["w_ffn2_up"], lp["w_ffn2_down"])
    if update_context:
        xc = xc + 0.5 * mc[8] * swiglu(modulate(rmsnorm(xc, lp["g_n3"]), mc[6], mc[7]), lp["w_ffn2_up"], lp["w_ffn2_down"])
    return x, xc


def setup_inputs(seed: int = 0) -> dict:
    key = jax.random.key(seed)
    ks = jax.random.split(key, 32)
    L, D, F = DEPTH, D_MODEL, FFN_HIDDEN

    def nrm(k, shape, scale=1.0):
        return scale * jax.random.normal(k, shape, jnp.float32)

    u = jax.random.uniform(ks[23], (L, 2, LRU_WIDTH), jnp.float32, minval=0.9, maxval=0.999)
    a = u ** (1.0 / LRU_C)
    lam = jnp.log(a) - jnp.log1p(-a)
    return {
        "x": nrm(ks[0], (BATCH, SEQ, D)),
        "c": nrm(ks[1], (BATCH, D)),
        "ctx": nrm(ks[2], (BATCH, CTX_LEN, D)),
        "c_ctx": nrm(ks[3], (D,)),
        "w_mod": nrm(ks[4], (L, D, N_MOD * D), 0.5 * D ** -0.5),
        "b_mod": nrm(ks[5], (L, N_MOD * D), 0.02),
        "g_n1": 1.0 + nrm(ks[6], (L, D), 0.1),
        "w_ffn1_up": nrm(ks[7], (L, D, 2 * F), D ** -0.5),
        "w_ffn1_down": nrm(ks[8], (L, F, D), F ** -0.5),
        "g_n2": 1.0 + nrm(ks[9], (L, D), 0.1),
        "w_in": nrm(ks[10], (L, D, D_IN), D ** -0.5),
        "b_in": nrm(ks[11], (L, D_IN), 0.02),
        "w_dw": nrm(ks[12], (L, CONF_KERNEL, CONF_WIDTH), CONF_KERNEL ** -0.5),
        "b_dw": nrm(ks[13], (L, CONF_WIDTH), 0.02),
        "g_ln": 1.0 + nrm(ks[14], (L, CONF_WIDTH), 0.1),
        "b_ln": nrm(ks[15], (L, CONF_WIDTH), 0.02),
        "w_conf_out": nrm(ks[16], (L, CONF_WIDTH, D), CONF_WIDTH ** -0.5),
        "w_lru_conv": nrm(ks[17], (L, LRU_CONV, LRU_WIDTH), LRU_CONV ** -0.5),
        "b_lru_conv": nrm(ks[18], (L, LRU_WIDTH), 0.02),
        "w_rec_gate": nrm(ks[19], (L, 2, LRU_BLOCKS, LRU_BLOCK_W, LRU_BLOCK_W), LRU_BLOCK_W ** -0.5),
        "b_rec_gate": nrm(ks[20], (L, 2, LRU_WIDTH), 0.02),
        "w_in_gate": nrm(ks[21], (L, 2, LRU_BLOCKS, LRU_BLOCK_W, LRU_BLOCK_W), LRU_BLOCK_W ** -0.5),
        "b_in_gate": nrm(ks[22], (L, 2, LRU_WIDTH), 0.02),
        "lru_lambda": lam,
        "w_lru_out": nrm(ks[24], (L, LRU_WIDTH, D), LRU_WIDTH ** -0.5),
        "w_out": nrm(ks[25], (L, D, D), D ** -0.5),
        "g_n3": 1.0 + nrm(ks[26], (L, D), 0.1),
        "w_ffn2_up": nrm(ks[27], (L, D, 2 * F), D ** -0.5),
        "w_ffn2_down": nrm(ks[28], (L, F, D), F ** -0.5),
        "g_final": 1.0 + nrm(ks[29], (D,), 0.1),
    }


def reference(x, c, ctx, c_ctx, w_mod, b_mod, g_n1, w_ffn1_up, w_ffn1_down, g_n2, w_in, b_in,
              w_dw, b_dw, g_ln, b_ln, w_conf_out, w_lru_conv, b_lru_conv, w_rec_gate, b_rec_gate,
              w_in_gate, b_in_gate, lru_lambda, w_lru_out, w_out, g_n3, w_ffn2_up, w_ffn2_down,
              g_final):
    x = x + grid_pos_embedding(x.shape[1], x.shape[2]).astype(x.dtype)[None]
    xc = ctx
    for i in range(DEPTH):
        lp = dict(w_mod=w_mod[i], b_mod=b_mod[i], g_n1=g_n1[i], w_ffn1_up=w_ffn1_up[i],
                  w_ffn1_down=w_ffn1_down[i], g_n2=g_n2[i], w_in=w_in[i], b_in=b_in[i],
                  w_dw=w_dw[i], b_dw=b_dw[i], g_ln=g_ln[i], b_ln=b_ln[i], w_conf_out=w_conf_out[i],
                  w_lru_conv=w_lru_conv[i], b_lru_conv=b_lru_conv[i], w_rec_gate=w_rec_gate[i],
                  b_rec_gate=b_rec_gate[i], w_in_gate=w_in_gate[i], b_in_gate=b_in_gate[i],
                  lru_lambda=lru_lambda[i], w_lru_out=w_lru_out[i], w_out=w_out[i], g_n3=g_n3[i],
                  w_ffn2_up=w_ffn2_up[i], w_ffn2_down=w_ffn2_down[i])
        x, xc = layer(x, xc, c, c_ctx, lp, i < DEPTH - 1)
    y = rmsnorm(x, g_final)
    return y
```

```python
import functools

import jax
import jax.numpy as jnp
from jax import lax
from jax.experimental import pallas as pl
from jax.experimental.pallas import tpu as pltpu

EPS = 1e-6
LRU_C = 8.0
GRID_W = 64
N_MOD = 9
SUBLANES = 8
LANES = 128
LRU_HALO_PREV = 16
LRU_HALO_NEXT = 8
VMEM_LIMIT_BYTES = 56 * 1024 * 1024

_BF16 = jnp.bfloat16
_F32 = jnp.float32


def _sigmoid(x):
    return 0.5 * jnp.tanh(0.5 * x) + 0.5


def _gelu_tanh(x):
    return 0.5 * x * (1.0 + jnp.tanh(0.7978845608028654 * (x + 0.044715 * (x * x * x))))


def _rms_mod(x, g, shift, scale):
    ms = jnp.mean(x * x, axis=-1, keepdims=True)
    return (x * lax.rsqrt(ms + EPS) * g) * (1.0 + scale) + shift


def _mod_slice(m_ref, idx, d):
    return m_ref[:, idx * d:(idx + 1) * d]


def _params(n_grid):
    return pltpu.CompilerParams(dimension_semantics=("arbitrary",) * n_grid,
                                vmem_limit_bytes=VMEM_LIMIT_BYTES)


def _mod_kernel(c_ref, w_ref, b_ref, o_ref):
    c = c_ref[...]
    a = (c * _sigmoid(c)).astype(_BF16)
    o_ref[...] = jnp.dot(a, w_ref[...].astype(_BF16), preferred_element_type=_F32) + b_ref[...]


def _modulation(cc, w_mod, b_mod):
    rows, d = cc.shape
    n = w_mod.shape[1]
    tn = d
    return pl.pallas_call(
        _mod_kernel,
        grid=(n // tn,),
        in_specs=[pl.BlockSpec((rows, d), lambda j: (0, 0)),
                  pl.BlockSpec((d, tn), lambda j: (0, j)),
                  pl.BlockSpec((1, tn), lambda j: (0, j))],
        out_specs=pl.BlockSpec((rows, tn), lambda j: (0, j)),
        out_shape=jax.ShapeDtypeStruct((rows, n), _F32),
        compiler_params=_params(1),
        name="modulation",
    )(cc, w_mod, b_mod.reshape(1, n))


def _ffn_kernel(*refs, d, f, fc, il_in, il_out, add_pos, final_norm, mod_idx):
    it = iter(refs)
    x_ref = next(it)
    pos_ref = next(it) if add_pos else None
    m_ref = next(it)
    g_ref = next(it)
    wu_ref = next(it)
    wd_ref = next(it)
    gf_ref = next(it) if final_norm else None
    o_ref = next(it)

    if il_in:
        x = jnp.concatenate([x_ref[:, s * d:(s + 1) * d] for s in range(SUBLANES)], axis=0)
    else:
        x = x_ref[...]
    if add_pos:
        x = x + pos_ref[...]
    shift, scale, gate = (_mod_slice(m_ref, i, d) for i in mod_idx)
    hb = _rms_mod(x, g_ref[...], shift, scale).astype(_BF16)

    acc = None
    for c0 in range(0, f, fc):
        gt = jnp.dot(hb, wu_ref[:, c0:c0 + fc], preferred_element_type=_F32)
        up = jnp.dot(hb, wu_ref[:, f + c0:f + c0 + fc], preferred_element_type=_F32)
        hg = 0.5 * gt
        act = ((hg + hg * jnp.tanh(hg)) * up).astype(_BF16)
        part = jnp.dot(act, wd_ref[c0:c0 + fc, :], preferred_element_type=_F32)
        acc = part if acc is None else acc + part
    y = x + (0.5 * gate) * acc
    if final_norm:
        ms = jnp.mean(y * y, axis=-1, keepdims=True)
        y = y * lax.rsqrt(ms + EPS) * gf_ref[...]
    if il_out:
        lc = y.shape[0] // SUBLANES
        for s in range(SUBLANES):
            o_ref[:, s * d:(s + 1) * d] = y[s * lc:(s + 1) * lc, :]
    else:
        o_ref[...] = y


def _ffn(x, m3, g, wu, wd, *, tile, il_in, il_out, mod_idx, pos=None, g_final=None, fc=256):
    d = g.shape[-1]
    f = wd.shape[0]
    b = x.shape[0]
    n_t = x.shape[1] if il_in else x.shape[1] // tile
    lc = tile // SUBLANES
    il_block = pl.BlockSpec((None, None, lc, SUBLANES * d), lambda t, i: (i, t, 0, 0))
    nat_block = pl.BlockSpec((None, tile, d), lambda t, i: (i, t, 0))
    in_specs = [il_block if il_in else nat_block]
    args = [x]
    if pos is not None:
        in_specs.append(pl.BlockSpec((tile, d), lambda t, i: (t, 0)))
        args.append(pos)
    in_specs += [pl.BlockSpec((None, 1, N_MOD * d), lambda t, i: (i, 0, 0)),
                 pl.BlockSpec((1, d), lambda t, i: (0, 0)),
                 pl.BlockSpec(wu.shape, lambda t, i: (0, 0)),
                 pl.BlockSpec(wd.shape, lambda t, i: (0, 0))]
    args += [m3, g.reshape(1, d), wu, wd]
    if g_final is not None:
        in_specs.append(pl.BlockSpec((1, d), lambda t, i: (0, 0)))
        args.append(g_final.reshape(1, d))
    if il_out:
        out_shape = jax.ShapeDtypeStruct((b, n_t, lc, SUBLANES * d), _F32)
    else:
        out_shape = jax.ShapeDtypeStruct((b, n_t * tile, d), _F32)
    kern = functools.partial(_ffn_kernel, d=d, f=f, fc=fc, il_in=il_in, il_out=il_out,
                             add_pos=pos is not None, final_norm=g_final is not None,
                             mod_idx=mod_idx)
    return pl.pallas_call(
        kern, grid=(n_t, b), in_specs=in_specs,
        out_specs=il_block if il_out else nat_block,
        out_shape=out_shape, compiler_params=_params(2), name="ffn",
    )(*args)


def _inproj_kernel(x_ref, m_ref, g_ref, w_ref, b_ref, *out_refs, d, dc, wl, cw, lru_only, mod_idx):
    shift, scale = (_mod_slice(m_ref, i, d) for i in mod_idx)
    hb = _rms_mod(x_ref[...], g_ref[...], shift, scale).astype(_BF16)

    def proj(c0):
        return jnp.dot(hb, w_ref[:, c0:c0 + cw], preferred_element_type=_F32) + b_ref[:, c0:c0 + cw]

    if lru_only:
        (ux_ref,) = out_refs
        for c0 in range(0, wl, cw):
            ux_ref[:, c0:c0 + cw] = proj(c0)
        return
    u_ref, ux_ref, gl_ref, gt_ref = out_refs
    for c0 in range(0, dc, cw):
        u_ref[:, c0:c0 + cw] = (proj(c0) * _sigmoid(proj(dc + c0))).astype(_BF16)
    for c0 in range(0, wl, cw):
        ux_ref[:, c0:c0 + cw] = proj(2 * dc + c0)
        gl_ref[:, c0:c0 + cw] = _gelu_tanh(proj(2 * dc + wl + c0)).astype(_BF16)
    for c0 in range(0, 2 * d, cw):
        gt_ref[:, c0:c0 + cw] = _sigmoid(proj(2 * dc + 2 * wl + c0)).astype(_BF16)


def _inproj(x, m3, g, w, bias, *, tm, dc, wl, lru_only, mod_idx, cw=256):
    b, t, d = x.shape
    n = w.shape[1]
    row = lambda width: pl.BlockSpec((None, tm, width), lambda i, j: (i, j, 0))
    if lru_only:
        out_specs = [row(wl)]
        out_shape = [jax.ShapeDtypeStruct((b, t, wl), _F32)]
    else:
        out_specs = [row(dc), row(wl), row(wl), row(2 * d)]
        out_shape = [jax.ShapeDtypeStruct((b, t, dc), _BF16),
                     jax.ShapeDtypeStruct((b, t, wl), _F32),
                     jax.ShapeDtypeStruct((b, t, wl), _BF16),
                     jax.ShapeDtypeStruct((b, t, 2 * d), _BF16)]
    kern = functools.partial(_inproj_kernel, d=d, dc=dc, wl=wl, cw=cw, lru_only=lru_only,
                             mod_idx=mod_idx)
    return pl.pallas_call(
        kern, grid=(b, t // tm),
        in_specs=[row(d),
                  pl.BlockSpec((None, 1, N_MOD * d), lambda i, j: (i, 0, 0)),
                  pl.BlockSpec((1, d), lambda i, j: (0, 0)),
                  pl.BlockSpec((d, n), lambda i, j: (0, 0)),
                  pl.BlockSpec((1, n), lambda i, j: (0, 0))],
        out_specs=out_specs, out_shape=out_shape,
        compiler_params=_params(2), name="inproj",
    )(x, m3, g.reshape(1, d), w, bias.reshape(1, n))


def _lru_kernel(*refs, tile, wl, n_blocks, reverse, combine):
    it = iter(refs)
    ux_ref, pv_ref, nx_ref = next(it), next(it), next(it)
    cw_ref, cb_ref, wg_ref, bg_ref, lam_ref, h0_ref = (next(it) for _ in range(6))
    hf_ref = next(it) if combine else None
    gl_ref = next(it) if combine else None
    out_ref, hlast_ref, e_ref, carry_ref = next(it), next(it), next(it), next(it)

    n_t = pl.num_programs(1)
    i = pl.program_id(1)
    tt = n_t - 1 - i if reverse else i
    lc = tile // SUBLANES
    hp, hn = LRU_HALO_PREV, LRU_HALO_NEXT

    @pl.when(i == 0)
    def _():
        carry_ref[...] = jnp.broadcast_to(h0_ref[...], carry_ref.shape)

    sub_p = lax.broadcasted_iota(jnp.int32, (hp, wl), 0) % SUBLANES
    pv = jnp.where(tt == 0, 0.0, pv_ref[...])
    e_ref[0:hp, :] = jnp.where(sub_p == 0, pltpu.roll(pv, hp - 7, 0),
                               pltpu.roll(ux_ref[tile - hp:tile, :], 1, 0))
    e_ref[hp:hp + tile, :] = ux_ref[...]
    sub_n = lax.broadcasted_iota(jnp.int32, (hn, wl), 0)
    nx = jnp.where(tt == n_t - 1, 0.0, nx_ref[...])
    e_ref[hp + tile:hp + tile + hn, :] = jnp.where(sub_n == 7, pltpu.roll(nx, 7, 0),
                                                   pltpu.roll(ux_ref[0:hn, :], 7, 0))

    sub = lax.broadcasted_iota(jnp.int32, (SUBLANES, LANES), 0)
    for blk in range(n_blocks):
        ln = slice(blk * LANES, (blk + 1) * LANES)
        xr = cb_ref[:, ln]
        for k in range(4):
            xr = xr + cw_ref[k:k + 1, ln] * e_ref[SUBLANES * k:SUBLANES * k + tile, ln]
        z = jnp.dot(xr.astype(_BF16), wg_ref[blk], preferred_element_type=_F32) + bg_ref[blk]
        r = _sigmoid(z[:, :LANES])
        ig = _sigmoid(z[:, LANES:])
        y = -lam_ref[:, ln]
        softplus = jnp.maximum(y, 0.0) + jnp.log1p(jnp.exp(-jnp.abs(y)))
        a = jnp.exp(r * (-LRU_C * softplus))
        bb = jnp.sqrt(1.0 - a * a) * (ig * xr)

        h = jnp.zeros((SUBLANES, LANES), _F32)
        p = jnp.ones((SUBLANES, LANES), _F32)
        hs, ps = [None] * lc, [None] * lc
        order = range(lc - 1, -1, -1) if reverse else range(lc)
        for j in order:
            aj = a[SUBLANES * j:SUBLANES * (j + 1), :]
            h = aj * h + bb[SUBLANES * j:SUBLANES * (j + 1), :]
            p = aj * p
            hs[j], ps[j] = h, p
        hin = carry_ref[:, ln]
        shift = 7 if reverse else 1
        for s in (range(6, -1, -1) if reverse else range(1, SUBLANES)):
            hin = jnp.where(sub == s, pltpu.roll(p * hin + h, shift, 0), hin)
        fin = p * hin + h
        carry_ref[:, ln] = pltpu.roll(fin, shift, 0)
        hlast_ref[:, ln] = fin if reverse else pltpu.roll(fin, 1, 0)
        hfull = jnp.concatenate([hs[j] + ps[j] * hin for j in range(lc)], axis=0)
        if combine:
            out_ref[:, ln] = ((hf_ref[:, ln] + hfull) * gl_ref[:, ln].astype(_F32)).astype(_BF16)
        else:
            out_ref[:, ln] = hfull


def _lru(ux, cw, cb, wg, bg, lam, h0, *, tile, reverse, hf=None, gl=None):
    b, t, wl = ux.shape
    n_t = t // tile
    n_blocks = wl // LANES
    combine = hf is not None
    hp, hn = LRU_HALO_PREV, LRU_HALO_NEXT

    def tmap(j):
        return n_t - 1 - j if reverse else j

    main = lambda width: pl.BlockSpec((None, tile, width), lambda i, j: (i, tmap(j), 0))
    prev = pl.BlockSpec((None, hp, wl), lambda i, j: (i, jnp.maximum(tmap(j) * (tile // hp) - 1, 0), 0))
    nxt = pl.BlockSpec((None, hn, wl),
                       lambda i, j: (i, jnp.minimum((tmap(j) + 1) * (tile // hn), t // hn - 1), 0))
    full = lambda a: pl.BlockSpec(a.shape, lambda i, j: (0,) * a.ndim)
    in_specs = [main(wl), prev, nxt, full(cw), full(cb), full(wg), full(bg), full(lam),
                pl.BlockSpec((None, 1, wl), lambda i, j: (i, 0, 0))]
    args = [ux, ux, ux, cw, cb, wg, bg, lam, h0]
    if combine:
        in_specs += [main(wl), main(wl)]
        args += [hf, gl]
    kern = functools.partial(_lru_kernel, tile=tile, wl=wl, n_blocks=n_blocks,
                             reverse=reverse, combine=combine)
    return pl.pallas_call(
        kern, grid=(b, n_t), in_specs=in_specs,
        out_specs=[main(wl), pl.BlockSpec((None, SUBLANES, wl), lambda i, j: (i, 0, 0))],
        out_shape=[jax.ShapeDtypeStruct((b, t, wl), _BF16 if combine else _F32),
                   jax.ShapeDtypeStruct((b, SUBLANES, wl), _F32)],
        scratch_shapes=[pltpu.VMEM((hp + tile + hn, wl), _F32),
                        pltpu.VMEM((SUBLANES, wl), _F32)],
        compiler_params=_params(2), name="lru_bwd" if reverse else "lru_fwd",
    )(*args)


def _mix_kernel(u_ref, up_ref, un_ref, yl_ref, gt_ref, x_ref, m_ref, wdw_ref, bdw_ref,
                gln_ref, bln_ref, wco_ref, wlo_ref, wo_ref, o_ref, e_ref, yc_ref,
                *, tile, d, taps, halo, rows_per_step, gate_idx):
    n_t = pl.num_programs(1)
    i = pl.program_id(1)
    pad = taps // 2
    hb_ = halo // SUBLANES

    sub = lax.broadcasted_iota(jnp.int32, (halo, d), 0) % SUBLANES
    pv = jnp.where(i == 0, 0.0, up_ref[...].astype(_F32))
    e_ref[0:halo, :] = jnp.where(sub == 0, pltpu.roll(pv, halo - 7, 0),
                                 pltpu.roll(u_ref[tile - halo:tile, :].astype(_F32), 1, 0))
    e_ref[halo:halo + tile, :] = u_ref[...].astype(_F32)
    nx = jnp.where(i == n_t - 1, 0.0, un_ref[...].astype(_F32))
    e_ref[halo + tile:2 * halo + tile, :] = jnp.where(
        sub == 7, pltpu.roll(nx, 7, 0), pltpu.roll(u_ref[0:halo, :].astype(_F32), halo - 1, 0))

    def conv_step(gi, carry):
        r0 = pl.multiple_of(gi * rows_per_step, rows_per_step)
        for lg in range(d // LANES):
            ln = slice(lg * LANES, (lg + 1) * LANES)
            acc = bdw_ref[:, ln]
            for k in range(taps):
                off = SUBLANES * (hb_ - pad + k)
                acc = acc + wdw_ref[k:k + 1, ln] * e_ref[pl.ds(r0 + off, rows_per_step), ln]
            yc_ref[pl.ds(r0, rows_per_step), ln] = acc
        return carry

    lax.fori_loop(0, tile // rows_per_step, conv_step, 0)

    yc = yc_ref[...]
    mu = jnp.mean(yc, axis=-1, keepdims=True)
    xc = yc - mu
    ln_out = xc * lax.rsqrt(jnp.mean(xc * xc, axis=-1, keepdims=True) + EPS) * gln_ref[...] + bln_ref[...]
    act = (ln_out * _sigmoid(ln_out)).astype(_BF16)
    y_conf = jnp.dot(act, wco_ref[...], preferred_element_type=_F32)
    y_lru = jnp.dot(yl_ref[...], wlo_ref[...], preferred_element_type=_F32)
    mixed = (gt_ref[:, :d].astype(_F32) * y_conf + gt_ref[:, d:].astype(_F32) * y_lru).astype(_BF16)
    y = jnp.dot(mixed, wo_ref[...], preferred_element_type=_F32)
    o_ref[...] = x_ref[...] + _mod_slice(m_ref, gate_idx, d) * y


def _mix(u, yl, gates, x1, m3, w_dw, b_dw, g_ln, b_ln, wco, wlo, wo, *, tile, gate_idx,
         halo=128, rows_per_step=32):
    b, t, d = x1.shape
    wl = yl.shape[-1]
    taps = w_dw.shape[0]
    n_t = t // tile
    assert halo // SUBLANES >= taps // 2 + 1 and tile >= halo
    main = lambda width: pl.BlockSpec((None, tile, width), lambda i, j: (i, j, 0))
    prev = pl.BlockSpec((None, halo, d), lambda i, j: (i, jnp.maximum(j * (tile // halo) - 1, 0), 0))
    nxt = pl.BlockSpec((None, halo, d),
                       lambda i, j: (i, jnp.minimum((j + 1) * (tile // halo), t // halo - 1), 0))
    full = lambda a: pl.BlockSpec(a.shape, lambda i, j: (0,) * a.ndim)
    vec = lambda a: a.reshape(1, -1)
    consts = [w_dw, vec(b_dw), vec(g_ln), vec(b_ln), wco, wlo, wo]
    kern = functools.partial(_mix_kernel, tile=tile, d=d, taps=taps, halo=halo,
                             rows_per_step=rows_per_step, gate_idx=gate_idx)
    return pl.pallas_call(
        kern, grid=(b, n_t),
        in_specs=[main(d), prev, nxt, main(wl), main(2 * d), main(d),
                  pl.BlockSpec((None, 1, N_MOD * d), lambda i, j: (i, 0, 0))]
                 + [full(a) for a in consts],
        out_specs=main(d),
        out_shape=jax.ShapeDtypeStruct((b, t, d), _F32),
        scratch_shapes=[pltpu.VMEM((tile + 2 * halo, d), _F32), pltpu.VMEM((tile, d), _F32)],
        compiler_params=_params(2), name="mix",
    )(u, u, u, yl, gates, x1, m3, *consts)


def _grid_pos_embedding(seq_len, dim):
    rows = seq_len // GRID_W
    t = jnp.arange(rows * GRID_W)
    row = (t // GRID_W).astype(_F32)
    col = (t % GRID_W).astype(_F32)
    q = dim // 4
    omega = 1.0 / (10000.0 ** (jnp.arange(q, dtype=_F32) / q))
    er = row[:, None] * omega
    ec = col[:, None] * omega
    return jnp.concatenate([jnp.sin(er), jnp.cos(er), jnp.sin(ec), jnp.cos(ec)], axis=-1)


def _gate_weights(w_rec, b_rec, w_in, b_in, direction):
    n_blocks, bw, _ = w_rec.shape[1:]
    wg = jnp.concatenate([w_rec[direction], w_in[direction]], axis=-1).astype(_BF16)
    bg = jnp.concatenate([b_rec[direction].reshape(n_blocks, 1, bw),
                          b_in[direction].reshape(n_blocks, 1, bw)], axis=-1)
    return wg, bg


def _layer(x, c, ctx, c_ctx, lp, g_final, *, tile, ctx_tile):
    b, t, d = x.shape
    tc = ctx.shape[1]
    dc = lp["w_dw"].shape[-1]
    wl = lp["w_lru_conv"].shape[-1]
    col_lru = 2 * dc

    cc = jnp.concatenate([c, c_ctx[None, :]], axis=0)
    cc = jnp.pad(cc, ((0, SUBLANES - cc.shape[0] % SUBLANES), (0, 0)))
    m_all = _modulation(cc, lp["w_mod"], lp["b_mod"])
    m3 = m_all[:b, None, :]
    mc3 = jnp.broadcast_to(m_all[b][None, None, :], (b, 1, N_MOD * d))

    wu1, wd1 = lp["w_ffn1_up"].astype(_BF16), lp["w_ffn1_down"].astype(_BF16)
    wu2, wd2 = lp["w_ffn2_up"].astype(_BF16), lp["w_ffn2_down"].astype(_BF16)
    w_in = lp["w_in"].astype(_BF16)
    lam = lp["lru_lambda"]
    gates = [_gate_weights(lp["w_rec_gate"], lp["b_rec_gate"], lp["w_in_gate"], lp["b_in_gate"], k)
             for k in range(2)]
    cw, cb = lp["w_lru_conv"], lp["b_lru_conv"].reshape(1, wl)

    def lru_pair(ux, h0f, h0b, tl, hf_gl=None):
        hf, hf_last = _lru(ux, cw, cb, gates[0][0], gates[0][1], lam[0:1], h0f, tile=tl, reverse=False)
        extra = {} if hf_gl is None else dict(hf=hf, gl=hf_gl)
        out, hb_last = _lru(ux, cw, cb, gates[1][0], gates[1][1], lam[1:2], h0b, tile=tl,
                            reverse=True, **extra)
        return out, hf_last[:, 0:1, :], hb_last[:, 0:1, :]

    xc1 = _ffn(ctx, mc3, lp["g_n1"], wu1, wd1, tile=ctx_tile, il_in=False, il_out=True,
               mod_idx=(0, 1, 2)).reshape(b, tc, d)
    (uxc,) = _inproj(xc1, mc3, lp["g_n2"], w_in[:, col_lru:col_lru + wl],
                     lp["b_in"][col_lru:col_lru + wl], tm=ctx_tile, dc=dc, wl=wl,
                     lru_only=True, mod_idx=(3, 4))
    zeros = jnp.zeros((b, 1, wl), _F32)
    _, h0f, h0b = lru_pair(uxc, zeros, zeros, ctx_tile)

    pos = _grid_pos_embedding(t, d)
    x1 = _ffn(x, m3, lp["g_n1"], wu1, wd1, tile=tile, il_in=False, il_out=True,
              mod_idx=(0, 1, 2), pos=pos).reshape(b, t, d)
    u, ux, gl, gts = _inproj(x1, m3, lp["g_n2"], w_in, lp["b_in"], tm=tile, dc=dc, wl=wl,
                             lru_only=False, mod_idx=(3, 4))
    yl, _, _ = lru_pair(ux, h0f, h0b, tile, hf_gl=gl)
    x2 = _mix(u, yl, gts, x1, m3, lp["w_dw"], lp["b_dw"], lp["g_ln"], lp["b_ln"],
              lp["w_conf_out"].astype(_BF16), lp["w_lru_out"].astype(_BF16),
              lp["w_out"].astype(_BF16), tile=tile, gate_idx=5)
    x2 = x2.reshape(b, t // tile, tile // SUBLANES, SUBLANES * d)
    return _ffn(x2, m3, lp["g_n3"], wu2, wd2, tile=tile, il_in=True, il_out=False,
                mod_idx=(6, 7, 8), g_final=g_final)


def _forward(x, c, ctx, c_ctx, params, g_final, *, tile=512, ctx_tile=256):
    depth = params["w_mod"].shape[0]
    assert depth == 1, "only the single-layer (context read-only) block is implemented"
    lp = {k: v[0] for k, v in params.items()}
    return _layer(x, c, ctx, c_ctx, lp, g_final, tile=tile, ctx_tile=ctx_tile)


def kernel(x, c, ctx, c_ctx, w_mod, b_mod, g_n1, w_ffn1_up, w_ffn1_down, g_n2, w_in, b_in, w_dw, b_dw, g_ln, b_ln, w_conf_out, w_lru_conv, b_lru_conv, w_rec_gate, b_rec_gate, w_in_gate, b_in_gate, lru_lambda, w_lru_out, w_out, g_n3, w_ffn2_up, w_ffn2_down, g_final):
    params = dict(w_mod=w_mod, b_mod=b_mod, g_n1=g_n1, w_ffn1_up=w_ffn1_up, w_ffn1_down=w_ffn1_down,
                  g_n2=g_n2, w_in=w_in, b_in=b_in, w_dw=w_dw, b_dw=b_dw, g_ln=g_ln, b_ln=b_ln,
                  w_conf_out=w_conf_out, w_lru_conv=w_lru_conv, b_lru_conv=b_lru_conv,
                  w_rec_gate=w_rec_gate, b_rec_gate=b_rec_gate, w_in_gate=w_in_gate,
                  b_in_gate=b_in_gate, lru_lambda=lru_lambda, w_lru_out=w_lru_out, w_out=w_out,
                  g_n3=g_n3, w_ffn2_up=w_ffn2_up, w_ffn2_down=w_ffn2_down)
    return _forward(x, c, ctx, c_ctx, params, g_final)
```

```python
import functools

import jax
import jax.numpy as jnp
from jax import lax
from jax.experimental import pallas as pl
from jax.experimental.pallas import tpu as pltpu

EPS = 1e-6
LRU_C = 8.0
GRID_W = 64
N_MOD = 9
SUBLANES = 8
LANES = 128
LRU_HALO_PREV = 16
LRU_HALO_NEXT = 8
VMEM_LIMIT_BYTES = 56 * 1024 * 1024

_BF16 = jnp.bfloat16
_F32 = jnp.float32


def _sigmoid(x):
    return 0.5 * jnp.tanh(0.5 * x) + 0.5


def _gelu_tanh(x):
    return 0.5 * x * (1.0 + jnp.tanh(0.7978845608028654 * (x + 0.044715 * (x * x * x))))


def _rms_mod(x, g, shift, scale):
    ms = jnp.mean(x * x, axis=-1, keepdims=True)
    return (x * lax.rsqrt(ms + EPS) * g) * (1.0 + scale) + shift


def _mod_slice(m_ref, idx, d):
    return m_ref[:, idx * d:(idx + 1) * d]


def _params(n_grid):
    return pltpu.CompilerParams(dimension_semantics=("arbitrary",) * n_grid,
                                vmem_limit_bytes=VMEM_LIMIT_BYTES)


def _mod_kernel(c_ref, w_ref, b_ref, o_ref):
    c = c_ref[...]
    a = (c * _sigmoid(c)).astype(_BF16)
    o_ref[...] = jnp.dot(a, w_ref[...].astype(_BF16), preferred_element_type=_F32) + b_ref[...]


def _modulation(cc, w_mod, b_mod):
    rows, d = cc.shape
    n = w_mod.shape[1]
    tn = d
    return pl.pallas_call(
        _mod_kernel,
        grid=(n // tn,),
        in_specs=[pl.BlockSpec((rows, d), lambda j: (0, 0)),
                  pl.BlockSpec((d, tn), lambda j: (0, j)),
                  pl.BlockSpec((1, tn), lambda j: (0, j))],
        out_specs=pl.BlockSpec((rows, tn), lambda j: (0, j)),
        out_shape=jax.ShapeDtypeStruct((rows, n), _F32),
        compiler_params=_params(1),
        name="modulation",
    )(cc, w_mod, b_mod.reshape(1, n))


def _ffn_kernel(*refs, d, f, fc, il_in, il_out, add_pos, final_norm, mod_idx):
    it = iter(refs)
    x_ref = next(it)
    pos_ref = next(it) if add_pos else None
    m_ref = next(it)
    g_ref = next(it)
    wu_ref = next(it)
    wd_ref = next(it)
    gf_ref = next(it) if final_norm else None
    o_ref = next(it)

    tile = x_ref.shape[0]
    lc = tile // SUBLANES
    x = x_ref[...]
    if il_in:
        x = jnp.swapaxes(x.reshape(lc, SUBLANES, d), 0, 1).reshape(tile, d)
    if add_pos:
        x = x + pos_ref[...]
    shift, scale, gate = (_mod_slice(m_ref, i, d) for i in mod_idx)
    hb = _rms_mod(x, g_ref[...], shift, scale).astype(_BF16)

    acc = None
    for c0 in range(0, f, fc):
        gt = jnp.dot(hb, wu_ref[:, c0:c0 + fc], preferred_element_type=_F32)
        up = jnp.dot(hb, wu_ref[:, f + c0:f + c0 + fc], preferred_element_type=_F32)
        hg = 0.5 * gt
        act = ((hg + hg * jnp.tanh(hg)) * up).astype(_BF16)
        part = jnp.dot(act, wd_ref[c0:c0 + fc, :], preferred_element_type=_F32)
        acc = part if acc is None else acc + part
    y = x + (0.5 * gate) * acc
    if final_norm:
        ms = jnp.mean(y * y, axis=-1, keepdims=True)
        y = y * lax.rsqrt(ms + EPS) * gf_ref[...]
    if il_out:
        y = jnp.swapaxes(y.reshape(SUBLANES, lc, d), 0, 1).reshape(tile, d)
    o_ref[...] = y


def _ffn(x, m3, g, wu, wd, *, tile, il_in, il_out, mod_idx, pos=None, g_final=None, fc=256):
    b, t_len, d = x.shape
    f = wd.shape[0]
    n_t = t_len // tile
    row_block = pl.BlockSpec((None, tile, d), lambda t, i: (i, t, 0))
    in_specs = [row_block]
    args = [x]
    if pos is not None:
        in_specs.append(pl.BlockSpec((tile, d), lambda t, i: (t, 0)))
        args.append(pos)
    in_specs += [pl.BlockSpec((None, 1, N_MOD * d), lambda t, i: (i, 0, 0)),
                 pl.BlockSpec((1, d), lambda t, i: (0, 0)),
                 pl.BlockSpec(wu.shape, lambda t, i: (0, 0)),
                 pl.BlockSpec(wd.shape, lambda t, i: (0, 0))]
    args += [m3, g.reshape(1, d), wu, wd]
    if g_final is not None:
        in_specs.append(pl.BlockSpec((1, d), lambda t, i: (0, 0)))
        args.append(g_final.reshape(1, d))
    kern = functools.partial(_ffn_kernel, d=d, f=f, fc=fc, il_in=il_in, il_out=il_out,
                             add_pos=pos is not None, final_norm=g_final is not None,
                             mod_idx=mod_idx)
    return pl.pallas_call(
        kern, grid=(n_t, b), in_specs=in_specs, out_specs=row_block,
        out_shape=jax.ShapeDtypeStruct((b, t_len, d), _F32),
        compiler_params=_params(2), name="ffn",
    )(*args)


def _inproj_kernel(x_ref, m_ref, g_ref, w_ref, b_ref, *out_refs, d, dc, wl, cw, lru_only, mod_idx):
    shift, scale = (_mod_slice(m_ref, i, d) for i in mod_idx)
    hb = _rms_mod(x_ref[...], g_ref[...], shift, scale).astype(_BF16)

    def proj(c0):
        return jnp.dot(hb, w_ref[:, c0:c0 + cw], preferred_element_type=_F32) + b_ref[:, c0:c0 + cw]

    if lru_only:
        (ux_ref,) = out_refs
        for c0 in range(0, wl, cw):
            ux_ref[:, c0:c0 + cw] = proj(c0)
        return
    u_ref, ux_ref, gl_ref, gt_ref = out_refs
    for c0 in range(0, dc, cw):
        u_ref[:, c0:c0 + cw] = (proj(c0) * _sigmoid(proj(dc + c0))).astype(_BF16)
    for c0 in range(0, wl, cw):
        ux_ref[:, c0:c0 + cw] = proj(2 * dc + c0)
        gl_ref[:, c0:c0 + cw] = _gelu_tanh(proj(2 * dc + wl + c0)).astype(_BF16)
    for c0 in range(0, 2 * d, cw):
        gt_ref[:, c0:c0 + cw] = _sigmoid(proj(2 * dc + 2 * wl + c0)).astype(_BF16)


def _inproj(x, m3, g, w, bias, *, tm, dc, wl, lru_only, mod_idx, cw=256):
    b, t, d = x.shape
    n = w.shape[1]
    row = lambda width: pl.BlockSpec((None, tm, width), lambda i, j: (i, j, 0))
    if lru_only:
        out_specs = [row(wl)]
        out_shape = [jax.ShapeDtypeStruct((b, t, wl), _F32)]
    else:
        out_specs = [row(dc), row(wl), row(wl), row(2 * d)]
        out_shape = [jax.ShapeDtypeStruct((b, t, dc), _BF16),
                     jax.ShapeDtypeStruct((b, t, wl), _F32),
                     jax.ShapeDtypeStruct((b, t, wl), _BF16),
                     jax.ShapeDtypeStruct((b, t, 2 * d), _BF16)]
    kern = functools.partial(_inproj_kernel, d=d, dc=dc, wl=wl, cw=cw, lru_only=lru_only,
                             mod_idx=mod_idx)
    return pl.pallas_call(
        kern, grid=(b, t // tm),
        in_specs=[row(d),
                  pl.BlockSpec((None, 1, N_MOD * d), lambda i, j: (i, 0, 0)),
                  pl.BlockSpec((1, d), lambda i, j: (0, 0)),
                  pl.BlockSpec((d, n), lambda i, j: (0, 0)),
                  pl.BlockSpec((1, n), lambda i, j: (0, 0))],
        out_specs=out_specs, out_shape=out_shape,
        compiler_params=_params(2), name="inproj",
    )(x, m3, g.reshape(1, d), w, bias.reshape(1, n))


def _lru_kernel(*refs, tile, wl, n_blocks, reverse, combine):
    it = iter(refs)
    ux_ref, pv_ref, nx_ref = next(it), next(it), next(it)
    cw_ref, cb_ref, wg_ref, bg_ref, lam_ref, h0_ref = (next(it) for _ in range(6))
    hf_ref = next(it) if combine else None
    gl_ref = next(it) if combine else None
    out_ref, hlast_ref, e_ref, carry_ref = next(it), next(it), next(it), next(it)

    n_t = pl.num_programs(1)
    i = pl.program_id(1)
    tt = n_t - 1 - i if reverse else i
    lc = tile // SUBLANES
    hp, hn = LRU_HALO_PREV, LRU_HALO_NEXT

    @pl.when(i == 0)
    def _():
        carry_ref[...] = jnp.broadcast_to(h0_ref[...], carry_ref.shape)

    sub_p = lax.broadcasted_iota(jnp.int32, (hp, wl), 0) % SUBLANES
    pv = jnp.where(tt == 0, 0.0, pv_ref[...])
    e_ref[0:hp, :] = jnp.where(sub_p == 0, pltpu.roll(pv, hp - 7, 0),
                               pltpu.roll(ux_ref[tile - hp:tile, :], 1, 0))
    e_ref[hp:hp + tile, :] = ux_ref[...]
    sub_n = lax.broadcasted_iota(jnp.int32, (hn, wl), 0)
    nx = jnp.where(tt == n_t - 1, 0.0, nx_ref[...])
    e_ref[hp + tile:hp + tile + hn, :] = jnp.where(sub_n == 7, pltpu.roll(nx, 7, 0),
                                                   pltpu.roll(ux_ref[0:hn, :], 7, 0))

    sub = lax.broadcasted_iota(jnp.int32, (SUBLANES, LANES), 0)
    for blk in range(n_blocks):
        ln = slice(blk * LANES, (blk + 1) * LANES)
        xr = cb_ref[:, ln]
        for k in range(4):
            xr = xr + cw_ref[k:k + 1, ln] * e_ref[SUBLANES * k:SUBLANES * k + tile, ln]
        z = jnp.dot(xr.astype(_BF16), wg_ref[blk], preferred_element_type=_F32) + bg_ref[blk]
        r = _sigmoid(z[:, :LANES])
        ig = _sigmoid(z[:, LANES:])
        y = -lam_ref[:, ln]
        softplus = jnp.maximum(y, 0.0) + jnp.log1p(jnp.exp(-jnp.abs(y)))
        a = jnp.exp(r * (-LRU_C * softplus))
        bb = jnp.sqrt(1.0 - a * a) * (ig * xr)

        h = jnp.zeros((SUBLANES, LANES), _F32)
        p = jnp.ones((SUBLANES, LANES), _F32)
        hs, ps = [None] * lc, [None] * lc
        order = range(lc - 1, -1, -1) if reverse else range(lc)
        for j in order:
            aj = a[SUBLANES * j:SUBLANES * (j + 1), :]
            h = aj * h + bb[SUBLANES * j:SUBLANES * (j + 1), :]
            p = aj * p
            hs[j], ps[j] = h, p
        hin = carry_ref[:, ln]
        shift = 7 if reverse else 1
        for s in (range(6, -1, -1) if reverse else range(1, SUBLANES)):
            hin = jnp.where(sub == s, pltpu.roll(p * hin + h, shift, 0), hin)
        fin = p * hin + h
        carry_ref[:, ln] = pltpu.roll(fin, shift, 0)
        hlast_ref[:, ln] = fin if reverse else pltpu.roll(fin, 1, 0)
        hfull = jnp.concatenate([hs[j] + ps[j] * hin for j in range(lc)], axis=0)
        if combine:
            out_ref[:, ln] = ((hf_ref[:, ln] + hfull) * gl_ref[:, ln].astype(_F32)).astype(_BF16)
        else:
            out_ref[:, ln] = hfull


def _lru(ux, cw, cb, wg, bg, lam, h0, *, tile, reverse, hf=None, gl=None):
    b, t, wl = ux.shape
    n_t = t // tile
    n_blocks = wl // LANES
    combine = hf is not None
    hp, hn = LRU_HALO_PREV, LRU_HALO_NEXT

    def tmap(j):
        return n_t - 1 - j if reverse else j

    main = lambda width: pl.BlockSpec((None, tile, width), lambda i, j: (i, tmap(j), 0))
    prev = pl.BlockSpec((None, hp, wl), lambda i, j: (i, jnp.maximum(tmap(j) * (tile // hp) - 1, 0), 0))
    nxt = pl.BlockSpec((None, hn, wl),
                       lambda i, j: (i, jnp.minimum((tmap(j) + 1) * (tile // hn), t // hn - 1), 0))
    full = lambda a: pl.BlockSpec(a.shape, lambda i, j: (0,) * a.ndim)
    in_specs = [main(wl), prev, nxt, full(cw), full(cb), full(wg), full(bg), full(lam),
                pl.BlockSpec((None, 1, wl), lambda i, j: (i, 0, 0))]
    args = [ux, ux, ux, cw, cb, wg, bg, lam, h0]
    if combine:
        in_specs += [main(wl), main(wl)]
        args += [hf, gl]
    kern = functools.partial(_lru_kernel, tile=tile, wl=wl, n_blocks=n_blocks,
                             reverse=reverse, combine=combine)
    return pl.pallas_call(
        kern, grid=(b, n_t), in_specs=in_specs,
        out_specs=[main(wl), pl.BlockSpec((None, SUBLANES, wl), lambda i, j: (i, 0, 0))],
        out_shape=[jax.ShapeDtypeStruct((b, t, wl), _BF16 if combine else _F32),
                   jax.ShapeDtypeStruct((b, SUBLANES, wl), _F32)],
        scratch_shapes=[pltpu.VMEM((hp + tile + hn, wl), _F32),
                        pltpu.VMEM((SUBLANES, wl), _F32)],
        compiler_params=_params(2), name="lru_bwd" if reverse else "lru_fwd",
    )(*args)


def _mix_kernel(u_ref, up_ref, un_ref, yl_ref, gt_ref, x_ref, m_ref, wdw_ref, bdw_ref,
                gln_ref, bln_ref, wco_ref, wlo_ref, wo_ref, o_ref, e_ref, yc_ref,
                *, tile, d, taps, halo, rows_per_step, gate_idx):
    n_t = pl.num_programs(1)
    i = pl.program_id(1)
    pad = taps // 2
    hb_ = halo // SUBLANES

    sub = lax.broadcasted_iota(jnp.int32, (halo, d), 0) % SUBLANES
    pv = jnp.where(i == 0, 0.0, up_ref[...].astype(_F32))
    e_ref[0:halo, :] = jnp.where(sub == 0, pltpu.roll(pv, halo - 7, 0),
                                 pltpu.roll(u_ref[tile - halo:tile, :].astype(_F32), 1, 0))
    e_ref[halo:halo + tile, :] = u_ref[...].astype(_F32)
    nx = jnp.where(i == n_t - 1, 0.0, un_ref[...].astype(_F32))
    e_ref[halo + tile:2 * halo + tile, :] = jnp.where(
        sub == 7, pltpu.roll(nx, 7, 0), pltpu.roll(u_ref[0:halo, :].astype(_F32), halo - 1, 0))

    def conv_step(gi, carry):
        r0 = pl.multiple_of(gi * rows_per_step, rows_per_step)
        for lg in range(d // LANES):
            ln = slice(lg * LANES, (lg + 1) * LANES)
            acc = bdw_ref[:, ln]
            for k in range(taps):
                off = SUBLANES * (hb_ - pad + k)
                acc = acc + wdw_ref[k:k + 1, ln] * e_ref[pl.ds(r0 + off, rows_per_step), ln]
            yc_ref[pl.ds(r0, rows_per_step), ln] = acc
        return carry

    lax.fori_loop(0, tile // rows_per_step, conv_step, 0)

    yc = yc_ref[...]
    mu = jnp.mean(yc, axis=-1, keepdims=True)
    xc = yc - mu
    ln_out = xc * lax.rsqrt(jnp.mean(xc * xc, axis=-1, keepdims=True) + EPS) * gln_ref[...] + bln_ref[...]
    act = (ln_out * _sigmoid(ln_out)).astype(_BF16)
    y_conf = jnp.dot(act, wco_ref[...], preferred_element_type=_F32)
    y_lru = jnp.dot(yl_ref[...], wlo_ref[...], preferred_element_type=_F32)
    mixed = (gt_ref[:, :d].astype(_F32) * y_conf + gt_ref[:, d:].astype(_F32) * y_lru).astype(_BF16)
    y = jnp.dot(mixed, wo_ref[...], preferred_element_type=_F32)
    o_ref[...] = x_ref[...] + _mod_slice(m_ref, gate_idx, d) * y


def _mix(u, yl, gates, x1, m3, w_dw, b_dw, g_ln, b_ln, wco, wlo, wo, *, tile, gate_idx,
         halo=128, rows_per_step=32):
    b, t, d = x1.shape
    wl = yl.shape[-1]
    taps = w_dw.shape[0]
    n_t = t // tile
    assert halo // SUBLANES >= taps // 2 + 1 and tile >= halo
    main = lambda width: pl.BlockSpec((None, tile, width), lambda i, j: (i, j, 0))
    prev = pl.BlockSpec((None, halo, d), lambda i, j: (i, jnp.maximum(j * (tile // halo) - 1, 0), 0))
    nxt = pl.BlockSpec((None, halo, d),
                       lambda i, j: (i, jnp.minimum((j + 1) * (tile // halo), t // halo - 1), 0))
    full = lambda a: pl.BlockSpec(a.shape, lambda i, j: (0,) * a.ndim)
    vec = lambda a: a.reshape(1, -1)
    consts = [w_dw, vec(b_dw), vec(g_ln), vec(b_ln), wco, wlo, wo]
    kern = functools.partial(_mix_kernel, tile=tile, d=d, taps=taps, halo=halo,
                             rows_per_step=rows_per_step, gate_idx=gate_idx)
    return pl.pallas_call(
        kern, grid=(b, n_t),
        in_specs=[main(d), prev, nxt, main(wl), main(2 * d), main(d),
                  pl.BlockSpec((None, 1, N_MOD * d), lambda i, j: (i, 0, 0))]
                 + [full(a) for a in consts],
        out_specs=main(d),
        out_shape=jax.ShapeDtypeStruct((b, t, d), _F32),
        scratch_shapes=[pltpu.VMEM((tile + 2 * halo, d), _F32), pltpu.VMEM((tile, d), _F32)],
        compiler_params=_params(2), name="mix",
    )(u, u, u, yl, gates, x1, m3, *consts)


def _grid_pos_embedding(seq_len, dim):
    rows = seq_len // GRID_W
    t = jnp.arange(rows * GRID_W)
    row = (t // GRID_W).astype(_F32)
    col = (t % GRID_W).astype(_F32)
    q = dim // 4
    omega = 1.0 / (10000.0 ** (jnp.arange(q, dtype=_F32) / q))
    er = row[:, None] * omega
    ec = col[:, None] * omega
    return jnp.concatenate([jnp.sin(er), jnp.cos(er), jnp.sin(ec), jnp.cos(ec)], axis=-1)


def _gate_weights(w_rec, b_rec, w_in, b_in, direction):
    n_blocks, bw, _ = w_rec.shape[1:]
    wg = jnp.concatenate([w_rec[direction], w_in[direction]], axis=-1).astype(_BF16)
    bg = jnp.concatenate([b_rec[direction].reshape(n_blocks, 1, bw),
                          b_in[direction].reshape(n_blocks, 1, bw)], axis=-1)
    return wg, bg


def _layer(x, c, ctx, c_ctx, lp, g_final, *, tile, ctx_tile):
    b, t, d = x.shape
    tc = ctx.shape[1]
    dc = lp["w_dw"].shape[-1]
    wl = lp["w_lru_conv"].shape[-1]
    col_lru = 2 * dc

    cc = jnp.concatenate([c, c_ctx[None, :]], axis=0)
    cc = jnp.pad(cc, ((0, SUBLANES - cc.shape[0] % SUBLANES), (0, 0)))
    m_all = _modulation(cc, lp["w_mod"], lp["b_mod"])
    m3 = m_all[:b, None, :]
    mc3 = jnp.broadcast_to(m_all[b][None, None, :], (b, 1, N_MOD * d))

    wu1, wd1 = lp["w_ffn1_up"].astype(_BF16), lp["w_ffn1_down"].astype(_BF16)
    wu2, wd2 = lp["w_ffn2_up"].astype(_BF16), lp["w_ffn2_down"].astype(_BF16)
    w_in = lp["w_in"].astype(_BF16)
    lam = lp["lru_lambda"]
    gates = [_gate_weights(lp["w_rec_gate"], lp["b_rec_gate"], lp["w_in_gate"], lp["b_in_gate"], k)
             for k in range(2)]
    cw, cb = lp["w_lru_conv"], lp["b_lru_conv"].reshape(1, wl)

    def lru_pair(ux, h0f, h0b, tl, hf_gl=None):
        hf, hf_last = _lru(ux, cw, cb, gates[0][0], gates[0][1], lam[0:1], h0f, tile=tl, reverse=False)
        extra = {} if hf_gl is None else dict(hf=hf, gl=hf_gl)
        out, hb_last = _lru(ux, cw, cb, gates[1][0], gates[1][1], lam[1:2], h0b, tile=tl,
                            reverse=True, **extra)
        return out, hf_last[:, 0:1, :], hb_last[:, 0:1, :]

    xc1 = _ffn(ctx, mc3, lp["g_n1"], wu1, wd1, tile=ctx_tile, il_in=False, il_out=True,
               mod_idx=(0, 1, 2))
    (uxc,) = _inproj(xc1, mc3, lp["g_n2"], w_in[:, col_lru:col_lru + wl],
                     lp["b_in"][col_lru:col_lru + wl], tm=ctx_tile, dc=dc, wl=wl,
                     lru_only=True, mod_idx=(3, 4))
    zeros = jnp.zeros((b, 1, wl), _F32)
    _, h0f, h0b = lru_pair(uxc, zeros, zeros, ctx_tile)

    pos = _grid_pos_embedding(t, d)
    x1 = _ffn(x, m3, lp["g_n1"], wu1, wd1, tile=tile, il_in=False, il_out=True,
              mod_idx=(0, 1, 2), pos=pos)
    u, ux, gl, gts = _inproj(x1, m3, lp["g_n2"], w_in, lp["b_in"], tm=tile, dc=dc, wl=wl,
                             lru_only=False, mod_idx=(3, 4))
    yl, _, _ = lru_pair(ux, h0f, h0b, tile, hf_gl=gl)
    x2 = _mix(u, yl, gts, x1, m3, lp["w_dw"], lp["b_dw"], lp["g_ln"], lp["b_ln"],
              lp["w_conf_out"].astype(_BF16), lp["w_lru_out"].astype(_BF16),
              lp["w_out"].astype(_BF16), tile=tile, gate_idx=5)
    return _ffn(x2, m3, lp["g_n3"], wu2, wd2, tile=tile, il_in=True, il_out=False,
                mod_idx=(6, 7, 8), g_final=g_final)


def _forward(x, c, ctx, c_ctx, params, g_final, *, tile=512, ctx_tile=256):
    depth = params["w_mod"].shape[0]
    assert depth == 1, "only the single-layer (context read-only) block is implemented"
    lp = {k: v[0] for k, v in params.items()}
    return _layer(x, c, ctx, c_ctx, lp, g_final, tile=tile, ctx_tile=ctx_tile)


def kernel(x, c, ctx, c_ctx, w_mod, b_mod, g_n1, w_ffn1_up, w_ffn1_down, g_n2, w_in, b_in, w_dw, b_dw, g_ln, b_ln, w_conf_out, w_lru_conv, b_lru_conv, w_rec_gate, b_rec_gate, w_in_gate, b_in_gate, lru_lambda, w_lru_out, w_out, g_n3, w_ffn2_up, w_ffn2_down, g_final):
    params = dict(w_mod=w_mod, b_mod=b_mod, g_n1=g_n1, w_ffn1_up=w_ffn1_up, w_ffn1_down=w_ffn1_down,
                  g_n2=g_n2, w_in=w_in, b_in=b_in, w_dw=w_dw, b_dw=b_dw, g_ln=g_ln, b_ln=b_ln,
                  w_conf_out=w_conf_out, w_lru_conv=w_lru_conv, b_lru_conv=b_lru_conv,
                  w_rec_gate=w_rec_gate, b_rec_gate=b_rec_gate, w_in_gate=w_in_gate,
                  b_in_gate=b_in_gate, lru_lambda=lru_lambda, w_lru_out=w_lru_out, w_out=w_out,
                  g_n3=g_n3, w_ffn2_up=w_ffn2_up, w_ffn2_down=w_ffn2_down)
    return _forward(x, c, ctx, c_ctx, params, g_final)
```

```python
import functools

import jax
import jax.numpy as jnp
from jax import lax
from jax.experimental import pallas as pl
from jax.experimental.pallas import tpu as pltpu

EPS = 1e-6
LRU_C = 8.0
LOG2_E = 1.4426950408889634
GRID_W = 64
N_MOD = 9
SUBLANES = 8
LANES = 128
LRU_HALO_PREV = 16
LRU_HALO_NEXT = 8
VMEM_LIMIT_BYTES = 56 * 1024 * 1024

_BF16 = jnp.bfloat16
_F32 = jnp.float32


def _sigmoid(x):
    return 0.5 * jnp.tanh(0.5 * x) + 0.5


def _half_gelu_tanh(x):
    return 0.25 * x * (1.0 + jnp.tanh(0.7978845608028654 * (x + 0.044715 * (x * x * x))))


def _rms_mod(x, g, shift, scale):
    ms = jnp.mean(x * x, axis=-1, keepdims=True)
    return (x * lax.rsqrt(ms + EPS) * g) * (1.0 + scale) + shift


def _mod_slice(m_ref, idx, d):
    return m_ref[:, idx * d:(idx + 1) * d]


def _params(n_grid):
    return pltpu.CompilerParams(dimension_semantics=("arbitrary",) * n_grid,
                                vmem_limit_bytes=VMEM_LIMIT_BYTES)


def _mod_kernel(c_ref, w_ref, b_ref, o_ref):
    c = c_ref[...]
    a = (c * _sigmoid(c)).astype(_BF16)
    o_ref[...] = jnp.dot(a, w_ref[...].astype(_BF16), preferred_element_type=_F32) + b_ref[...]


def _modulation(cc, w_mod, b_mod):
    rows, d = cc.shape
    n = w_mod.shape[1]
    tn = d
    return pl.pallas_call(
        _mod_kernel,
        grid=(n // tn,),
        in_specs=[pl.BlockSpec((rows, d), lambda j: (0, 0)),
                  pl.BlockSpec((d, tn), lambda j: (0, j)),
                  pl.BlockSpec((1, tn), lambda j: (0, j))],
        out_specs=pl.BlockSpec((rows, tn), lambda j: (0, j)),
        out_shape=jax.ShapeDtypeStruct((rows, n), _F32),
        compiler_params=_params(1),
        name="modulation",
    )(cc, w_mod, b_mod.reshape(1, n))


def _ffn_kernel(*refs, d, f, fc, il_in, il_out, add_pos, final_norm, mod_idx):
    it = iter(refs)
    x_ref = next(it)
    prow_ref, pcol_ref = (next(it), next(it)) if add_pos else (None, None)
    m_ref = next(it)
    g_ref = next(it)
    wu_ref = next(it)
    wd_ref = next(it)
    gf_ref = next(it) if final_norm else None
    o_ref = next(it)

    tile = x_ref.shape[0]
    lc = tile // SUBLANES
    x = x_ref[...]
    if il_in:
        x = jnp.swapaxes(x.reshape(lc, SUBLANES, d), 0, 1).reshape(tile, d)
    if add_pos:
        col_half = pcol_ref[...]
        x = x + jnp.concatenate(
            [jnp.concatenate([jnp.broadcast_to(prow_ref[q:q + 1, :], col_half.shape), col_half], axis=-1)
             for q in range(tile // GRID_W)], axis=0)
    shift, scale, gate = (_mod_slice(m_ref, i, d) for i in mod_idx)
    hb = _rms_mod(x, g_ref[...], shift, scale).astype(_BF16)

    acc = None
    for c0 in range(0, f, fc):
        gt = jnp.dot(hb, wu_ref[:, c0:c0 + fc], preferred_element_type=_F32)
        up = jnp.dot(hb, wu_ref[:, f + c0:f + c0 + fc], preferred_element_type=_F32)
        hg = 0.5 * gt
        act = ((hg + hg * jnp.tanh(hg)) * up).astype(_BF16)
        part = jnp.dot(act, wd_ref[c0:c0 + fc, :], preferred_element_type=_F32)
        acc = part if acc is None else acc + part
    y = x + (0.5 * gate) * acc
    if final_norm:
        ms = jnp.mean(y * y, axis=-1, keepdims=True)
        y = y * lax.rsqrt(ms + EPS) * gf_ref[...]
    if il_out:
        y = jnp.swapaxes(y.reshape(SUBLANES, lc, d), 0, 1).reshape(tile, d)
    o_ref[...] = y


def _ffn(x, m3, g, wu, wd, *, tile, il_in, il_out, mod_idx, pos=None, g_final=None, fc=256):
    b, t_len, d = x.shape
    f = wd.shape[0]
    n_t = t_len // tile
    row_block = pl.BlockSpec((None, tile, d), lambda t, i: (i, t, 0))
    in_specs = [row_block]
    args = [x]
    if pos is not None:
        assert tile % GRID_W == 0
        pos_row, pos_col = pos
        in_specs += [pl.BlockSpec((tile // GRID_W, d // 2), lambda t, i: (t, 0)),
                     pl.BlockSpec(pos_col.shape, lambda t, i: (0, 0))]
        args += [pos_row, pos_col]
    in_specs += [pl.BlockSpec((None, 1, N_MOD * d), lambda t, i: (i, 0, 0)),
                 pl.BlockSpec((1, d), lambda t, i: (0, 0)),
                 pl.BlockSpec(wu.shape, lambda t, i: (0, 0)),
                 pl.BlockSpec(wd.shape, lambda t, i: (0, 0))]
    args += [m3, g.reshape(1, d), wu, wd]
    if g_final is not None:
        in_specs.append(pl.BlockSpec((1, d), lambda t, i: (0, 0)))
        args.append(g_final.reshape(1, d))
    kern = functools.partial(_ffn_kernel, d=d, f=f, fc=fc, il_in=il_in, il_out=il_out,
                             add_pos=pos is not None, final_norm=g_final is not None,
                             mod_idx=mod_idx)
    return pl.pallas_call(
        kern, grid=(n_t, b), in_specs=in_specs, out_specs=row_block,
        out_shape=jax.ShapeDtypeStruct((b, t_len, d), _F32),
        compiler_params=_params(2), name="ffn",
    )(*args)


def _inproj_kernel(x_ref, m_ref, g_ref, w_ref, b_ref, *out_refs, d, dc, wl, cw, lru_only, mod_idx):
    shift, scale = (_mod_slice(m_ref, i, d) for i in mod_idx)
    hb = _rms_mod(x_ref[...], g_ref[...], shift, scale).astype(_BF16)

    def proj(c0):
        return jnp.dot(hb, w_ref[:, c0:c0 + cw], preferred_element_type=_F32) + b_ref[:, c0:c0 + cw]

    if lru_only:
        (ux_ref,) = out_refs
        for c0 in range(0, wl, cw):
            ux_ref[:, c0:c0 + cw] = proj(c0)
        return
    u_ref, ux_ref, gl_ref, gt_ref = out_refs
    for c0 in range(0, dc, cw):
        u_ref[:, c0:c0 + cw] = (proj(c0) * _sigmoid(proj(dc + c0))).astype(_BF16)
    for c0 in range(0, wl, cw):
        ux_ref[:, c0:c0 + cw] = proj(2 * dc + c0)
        gl_ref[:, c0:c0 + cw] = _half_gelu_tanh(proj(2 * dc + wl + c0)).astype(_BF16)
    for c0 in range(0, 2 * d, cw):
        gt_ref[:, c0:c0 + cw] = _sigmoid(proj(2 * dc + 2 * wl + c0)).astype(_BF16)


def _inproj(x, m3, g, w, bias, *, tm, dc, wl, lru_only, mod_idx, cw=256):
    b, t, d = x.shape
    n = w.shape[1]
    row = lambda width: pl.BlockSpec((None, tm, width), lambda i, j: (i, j, 0))
    if lru_only:
        out_specs = [row(wl)]
        out_shape = [jax.ShapeDtypeStruct((b, t, wl), _F32)]
    else:
        out_specs = [row(dc), row(wl), row(wl), row(2 * d)]
        out_shape = [jax.ShapeDtypeStruct((b, t, dc), _BF16),
                     jax.ShapeDtypeStruct((b, t, wl), _F32),
                     jax.ShapeDtypeStruct((b, t, wl), _BF16),
                     jax.ShapeDtypeStruct((b, t, 2 * d), _BF16)]
    kern = functools.partial(_inproj_kernel, d=d, dc=dc, wl=wl, cw=cw, lru_only=lru_only,
                             mod_idx=mod_idx)
    return pl.pallas_call(
        kern, grid=(b, t // tm),
        in_specs=[row(d),
                  pl.BlockSpec((None, 1, N_MOD * d), lambda i, j: (i, 0, 0)),
                  pl.BlockSpec((1, d), lambda i, j: (0, 0)),
                  pl.BlockSpec((d, n), lambda i, j: (0, 0)),
                  pl.BlockSpec((1, n), lambda i, j: (0, 0))],
        out_specs=out_specs, out_shape=out_shape,
        compiler_params=_params(2), name="inproj",
    )(x, m3, g.reshape(1, d), w, bias.reshape(1, n))


def _lru_kernel(*refs, tile, wl, n_blocks, reverse, combine):
    it = iter(refs)
    ux_ref, pv_ref, nx_ref = next(it), next(it), next(it)
    cw_ref, cb_ref, wg_ref, bg_ref, lam_ref, h0_ref = (next(it) for _ in range(6))
    hf_ref = next(it) if combine else None
    gl_ref = next(it) if combine else None
    out_ref, hlast_ref, e_ref, carry_ref = next(it), next(it), next(it), next(it)

    n_t = pl.num_programs(1)
    i = pl.program_id(1)
    tt = n_t - 1 - i if reverse else i
    lc = tile // SUBLANES
    hp, hn = LRU_HALO_PREV, LRU_HALO_NEXT

    @pl.when(i == 0)
    def _():
        carry_ref[...] = jnp.broadcast_to(h0_ref[...], carry_ref.shape)

    sub_p = lax.broadcasted_iota(jnp.int32, (hp, wl), 0) % SUBLANES
    pv = jnp.where(tt == 0, 0.0, pv_ref[...])
    e_ref[0:hp, :] = jnp.where(sub_p == 0, pltpu.roll(pv, hp - 7, 0),
                               pltpu.roll(ux_ref[tile - hp:tile, :], 1, 0))
    e_ref[hp:hp + tile, :] = ux_ref[...]
    sub_n = lax.broadcasted_iota(jnp.int32, (hn, wl), 0)
    nx = jnp.where(tt == n_t - 1, 0.0, nx_ref[...])
    e_ref[hp + tile:hp + tile + hn, :] = jnp.where(sub_n == 7, pltpu.roll(nx, 7, 0),
                                                   pltpu.roll(ux_ref[0:hn, :], 7, 0))

    sub = lax.broadcasted_iota(jnp.int32, (SUBLANES, LANES), 0)
    for blk in range(n_blocks):
        ln = slice(blk * LANES, (blk + 1) * LANES)
        xr = cb_ref[:, ln]
        for k in range(4):
            xr = xr + cw_ref[k:k + 1, ln] * e_ref[SUBLANES * k:SUBLANES * k + tile, ln]
        zh = jnp.dot(xr.astype(_BF16), wg_ref[blk], preferred_element_type=_F32) + bg_ref[blk]
        t_rec = jnp.tanh(zh[:, :LANES])
        t_in = jnp.tanh(zh[:, LANES:])
        y = -lam_ref[:, ln]
        softplus = jnp.maximum(y, 0.0) + jnp.log1p(jnp.exp(-jnp.abs(y)))
        half_c = (-0.5 * LRU_C * LOG2_E) * softplus
        a = jnp.exp2(t_rec * half_c + half_c)
        v = 1.0 - a * a
        root = jnp.where(v > 0.0, v * lax.rsqrt(v), 0.0)
        bb = root * ((t_in + 1.0) * xr)

        h = jnp.zeros((SUBLANES, LANES), _F32)
        p = jnp.ones((SUBLANES, LANES), _F32)
        hs, ps = [None] * lc, [None] * lc
        order = range(lc - 1, -1, -1) if reverse else range(lc)
        for j in order:
            aj = a[SUBLANES * j:SUBLANES * (j + 1), :]
            h = aj * h + bb[SUBLANES * j:SUBLANES * (j + 1), :]
            p = aj * p
            hs[j], ps[j] = h, p
        hin = carry_ref[:, ln]
        shift = 7 if reverse else 1
        for s in (range(6, -1, -1) if reverse else range(1, SUBLANES)):
            hin = jnp.where(sub == s, pltpu.roll(p * hin + h, shift, 0), hin)
        fin = p * hin + h
        carry_ref[:, ln] = pltpu.roll(fin, shift, 0)
        hlast_ref[:, ln] = fin if reverse else pltpu.roll(fin, 1, 0)
        hfull = jnp.concatenate([hs[j] + ps[j] * hin for j in range(lc)], axis=0)
        if combine:
            out_ref[:, ln] = ((hf_ref[:, ln] + hfull) * gl_ref[:, ln].astype(_F32)).astype(_BF16)
        else:
            out_ref[:, ln] = hfull


def _lru(ux, cw, cb, wg, bg, lam, h0, *, tile, reverse, hf=None, gl=None):
    b, t, wl = ux.shape
    n_t = t // tile
    n_blocks = wl // LANES
    combine = hf is not None
    hp, hn = LRU_HALO_PREV, LRU_HALO_NEXT

    def tmap(j):
        return n_t - 1 - j if reverse else j

    main = lambda width: pl.BlockSpec((None, tile, width), lambda i, j: (i, tmap(j), 0))
    prev = pl.BlockSpec((None, hp, wl), lambda i, j: (i, jnp.maximum(tmap(j) * (tile // hp) - 1, 0), 0))
    nxt = pl.BlockSpec((None, hn, wl),
                       lambda i, j: (i, jnp.minimum((tmap(j) + 1) * (tile // hn), t // hn - 1), 0))
    full = lambda a: pl.BlockSpec(a.shape, lambda i, j: (0,) * a.ndim)
    in_specs = [main(wl), prev, nxt, full(cw), full(cb), full(wg), full(bg), full(lam),
                pl.BlockSpec((None, 1, wl), lambda i, j: (i, 0, 0))]
    args = [ux, ux, ux, cw, cb, wg, bg, lam, h0]
    if combine:
        in_specs += [main(wl), main(wl)]
        args += [hf, gl]
    kern = functools.partial(_lru_kernel, tile=tile, wl=wl, n_blocks=n_blocks,
                             reverse=reverse, combine=combine)
    return pl.pallas_call(
        kern, grid=(b, n_t), in_specs=in_specs,
        out_specs=[main(wl), pl.BlockSpec((None, SUBLANES, wl), lambda i, j: (i, 0, 0))],
        out_shape=[jax.ShapeDtypeStruct((b, t, wl), _BF16 if combine else _F32),
                   jax.ShapeDtypeStruct((b, SUBLANES, wl), _F32)],
        scratch_shapes=[pltpu.VMEM((hp + tile + hn, wl), _F32),
                        pltpu.VMEM((SUBLANES, wl), _F32)],
        compiler_params=_params(2), name="lru_bwd" if reverse else "lru_fwd",
    )(*args)


def _mix_kernel(u_ref, up_ref, un_ref, yl_ref, gt_ref, x_ref, m_ref, wdw_ref, bdw_ref,
                gln_ref, bln_ref, wco_ref, wlo_ref, wo_ref, o_ref, e_ref, yc_ref,
                *, tile, d, taps, halo, group, gate_idx):
    n_t = pl.num_programs(1)
    i = pl.program_id(1)
    pad = taps // 2
    lc = tile // SUBLANES
    hb_ = halo // SUBLANES

    sub = lax.broadcasted_iota(jnp.int32, (halo, d), 0) % SUBLANES
    pv = jnp.where(i == 0, 0.0, up_ref[...].astype(_F32))
    e_ref[0:halo, :] = jnp.where(sub == 0, pltpu.roll(pv, halo - 7, 0),
                                 pltpu.roll(u_ref[tile - halo:tile, :].astype(_F32), 1, 0))
    e_ref[halo:halo + tile, :] = u_ref[...].astype(_F32)
    nx = jnp.where(i == n_t - 1, 0.0, un_ref[...].astype(_F32))
    e_ref[halo + tile:2 * halo + tile, :] = jnp.where(
        sub == 7, pltpu.roll(nx, 7, 0), pltpu.roll(u_ref[0:halo, :].astype(_F32), halo - 1, 0))

    rows = group * SUBLANES
    for lg in range(d // LANES):
        ln = slice(lg * LANES, (lg + 1) * LANES)
        w = [jnp.broadcast_to(wdw_ref[k:k + 1, ln], (SUBLANES, LANES)) for k in range(taps)]
        bias = jnp.broadcast_to(bdw_ref[:, ln], (SUBLANES, LANES))

        def conv_group(gi, carry, ln=ln, w=w, bias=bias):
            r0 = pl.multiple_of(gi * rows, rows)
            acc = [[bias] * group, [None] * group]
            for m in range(group + taps - 1):
                blk = e_ref[pl.ds(r0 + SUBLANES * (hb_ - pad + m), SUBLANES), ln]
                for j in range(max(0, m - taps + 1), min(group, m + 1)):
                    part, term = acc[(m - j) % 2], w[m - j] * blk
                    part[j] = term if part[j] is None else part[j] + term
            yc_ref[pl.ds(r0, rows), ln] = jnp.concatenate(
                [acc[0][j] + acc[1][j] for j in range(group)], axis=0)
            return carry

        lax.fori_loop(0, lc // group, conv_group, 0)

    yc = yc_ref[...]
    mu = jnp.mean(yc, axis=-1, keepdims=True)
    xc = yc - mu
    ln_out = xc * lax.rsqrt(jnp.mean(xc * xc, axis=-1, keepdims=True) + EPS) * gln_ref[...] + bln_ref[...]
    act = (ln_out * _sigmoid(ln_out)).astype(_BF16)
    y_conf = jnp.dot(act, wco_ref[...], preferred_element_type=_F32)
    y_lru = jnp.dot(yl_ref[...], wlo_ref[...], preferred_element_type=_F32)
    mixed = (gt_ref[:, :d].astype(_F32) * y_conf + gt_ref[:, d:].astype(_F32) * y_lru).astype(_BF16)
    y = jnp.dot(mixed, wo_ref[...], preferred_element_type=_F32)
    o_ref[...] = x_ref[...] + _mod_slice(m_ref, gate_idx, d) * y


def _mix(u, yl, gates, x1, m3, w_dw, b_dw, g_ln, b_ln, wco, wlo, wo, *, tile, gate_idx,
         halo=128, group=8):
    b, t, d = x1.shape
    wl = yl.shape[-1]
    taps = w_dw.shape[0]
    n_t = t // tile
    assert halo // SUBLANES >= taps // 2 + 1 and tile >= halo and (tile // SUBLANES) % group == 0
    main = lambda width: pl.BlockSpec((None, tile, width), lambda i, j: (i, j, 0))
    prev = pl.BlockSpec((None, halo, d), lambda i, j: (i, jnp.maximum(j * (tile // halo) - 1, 0), 0))
    nxt = pl.BlockSpec((None, halo, d),
                       lambda i, j: (i, jnp.minimum((j + 1) * (tile // halo), t // halo - 1), 0))
    full = lambda a: pl.BlockSpec(a.shape, lambda i, j: (0,) * a.ndim)
    vec = lambda a: a.reshape(1, -1)
    consts = [w_dw, vec(b_dw), vec(g_ln), vec(b_ln), wco, wlo, wo]
    kern = functools.partial(_mix_kernel, tile=tile, d=d, taps=taps, halo=halo, group=group,
                             gate_idx=gate_idx)
    return pl.pallas_call(
        kern, grid=(b, n_t),
        in_specs=[main(d), prev, nxt, main(wl), main(2 * d), main(d),
                  pl.BlockSpec((None, 1, N_MOD * d), lambda i, j: (i, 0, 0))]
                 + [full(a) for a in consts],
        out_specs=main(d),
        out_shape=jax.ShapeDtypeStruct((b, t, d), _F32),
        scratch_shapes=[pltpu.VMEM((tile + 2 * halo, d), _F32), pltpu.VMEM((tile, d), _F32)],
        compiler_params=_params(2), name="mix",
    )(u, u, u, yl, gates, x1, m3, *consts)


def _grid_pos_tables(seq_len, dim):
    q = dim // 4
    omega = 1.0 / (10000.0 ** (jnp.arange(q, dtype=_F32) / q))
    er = jnp.arange(seq_len // GRID_W).astype(_F32)[:, None] * omega
    ec = jnp.arange(GRID_W).astype(_F32)[:, None] * omega
    return (jnp.concatenate([jnp.sin(er), jnp.cos(er)], axis=-1),
            jnp.concatenate([jnp.sin(ec), jnp.cos(ec)], axis=-1))


def _gate_weights(w_rec, b_rec, w_in, b_in, direction):
    n_blocks, bw, _ = w_rec.shape[1:]
    wg = (0.5 * jnp.concatenate([w_rec[direction], w_in[direction]], axis=-1)).astype(_BF16)
    bg = 0.5 * jnp.concatenate([b_rec[direction].reshape(n_blocks, 1, bw),
                                b_in[direction].reshape(n_blocks, 1, bw)], axis=-1)
    return wg, bg


def _layer(x, c, ctx, c_ctx, lp, g_final, *, tile, ctx_tile):
    b, t, d = x.shape
    dc = lp["w_dw"].shape[-1]
    wl = lp["w_lru_conv"].shape[-1]
    col_lru = 2 * dc

    cc = jnp.concatenate([c, c_ctx[None, :]], axis=0)
    cc = jnp.pad(cc, ((0, SUBLANES - cc.shape[0] % SUBLANES), (0, 0)))
    m_all = _modulation(cc, lp["w_mod"], lp["b_mod"])
    m3 = m_all[:b, None, :]
    mc3 = jnp.broadcast_to(m_all[b][None, None, :], (b, 1, N_MOD * d))

    wu1, wd1 = lp["w_ffn1_up"].astype(_BF16), lp["w_ffn1_down"].astype(_BF16)
    wu2, wd2 = lp["w_ffn2_up"].astype(_BF16), lp["w_ffn2_down"].astype(_BF16)
    w_in = lp["w_in"].astype(_BF16)
    lam = lp["lru_lambda"]
    gates = [_gate_weights(lp["w_rec_gate"], lp["b_rec_gate"], lp["w_in_gate"], lp["b_in_gate"], k)
             for k in range(2)]
    cw, cb = lp["w_lru_conv"], lp["b_lru_conv"].reshape(1, wl)

    def lru_pair(ux, h0f, h0b, tl, hf_gl=None):
        hf, hf_last = _lru(ux, cw, cb, gates[0][0], gates[0][1], lam[0:1], h0f, tile=tl, reverse=False)
        extra = {} if hf_gl is None else dict(hf=hf, gl=hf_gl)
        out, hb_last = _lru(ux, cw, cb, gates[1][0], gates[1][1], lam[1:2], h0b, tile=tl,
                            reverse=True, **extra)
        return out, hf_last[:, 0:1, :], hb_last[:, 0:1, :]

    xc1 = _ffn(ctx, mc3, lp["g_n1"], wu1, wd1, tile=ctx_tile, il_in=False, il_out=True,
               mod_idx=(0, 1, 2))
    (uxc,) = _inproj(xc1, mc3, lp["g_n2"], w_in[:, col_lru:col_lru + wl],
                     lp["b_in"][col_lru:col_lru + wl], tm=ctx_tile, dc=dc, wl=wl,
                     lru_only=True, mod_idx=(3, 4))
    zeros = jnp.zeros((b, 1, wl), _F32)
    _, h0f, h0b = lru_pair(uxc, zeros, zeros, ctx_tile)

    pos = _grid_pos_tables(t, d)
    x1 = _ffn(x, m3, lp["g_n1"], wu1, wd1, tile=tile, il_in=False, il_out=True,
              mod_idx=(0, 1, 2), pos=pos)
    u, ux, gl, gts = _inproj(x1, m3, lp["g_n2"], w_in, lp["b_in"], tm=tile, dc=dc, wl=wl,
                             lru_only=False, mod_idx=(3, 4))
    yl, _, _ = lru_pair(ux, h0f, h0b, tile, hf_gl=gl)
    x2 = _mix(u, yl, gts, x1, m3, lp["w_dw"], lp["b_dw"], lp["g_ln"], lp["b_ln"],
              lp["w_conf_out"].astype(_BF16), lp["w_lru_out"].astype(_BF16),
              lp["w_out"].astype(_BF16), tile=tile, gate_idx=5)
    return _ffn(x2, m3, lp["g_n3"], wu2, wd2, tile=tile, il_in=True, il_out=False,
                mod_idx=(6, 7, 8), g_final=g_final)


def _forward(x, c, ctx, c_ctx, params, g_final, *, tile=512, ctx_tile=256):
    depth = params["w_mod"].shape[0]
    assert depth == 1, "only the single-layer (context read-only) block is implemented"
    lp = {k: v[0] for k, v in params.items()}
    return _layer(x, c, ctx, c_ctx, lp, g_final, tile=tile, ctx_tile=ctx_tile)


def kernel(x, c, ctx, c_ctx, w_mod, b_mod, g_n1, w_ffn1_up, w_ffn1_down, g_n2, w_in, b_in, w_dw, b_dw, g_ln, b_ln, w_conf_out, w_lru_conv, b_lru_conv, w_rec_gate, b_rec_gate, w_in_gate, b_in_gate, lru_lambda, w_lru_out, w_out, g_n3, w_ffn2_up, w_ffn2_down, g_final):
    params = dict(w_mod=w_mod, b_mod=b_mod, g_n1=g_n1, w_ffn1_up=w_ffn1_up, w_ffn1_down=w_ffn1_down,
                  g_n2=g_n2, w_in=w_in, b_in=b_in, w_dw=w_dw, b_dw=b_dw, g_ln=g_ln, b_ln=b_ln,
                  w_conf_out=w_conf_out, w_lru_conv=w_lru_conv, b_lru_conv=b_lru_conv,
                  w_rec_gate=w_rec_gate, b_rec_gate=b_rec_gate, w_in_gate=w_in_gate,
                  b_in_gate=b_in_gate, lru_lambda=lru_lambda, w_lru_out=w_lru_out, w_out=w_out,
                  g_n3=g_n3, w_ffn2_up=w_ffn2_up, w_ffn2_down=w_ffn2_down)
    return _forward(x, c, ctx, c_ctx, params, g_final)
```

```python
import functools

import jax
import jax.numpy as jnp
from jax import lax
from jax.experimental import pallas as pl
from jax.experimental.pallas import tpu as pltpu

EPS = 1e-6
LRU_C = 8.0
LOG2_E = 1.4426950408889634
GRID_W = 64
N_MOD = 9
SUBLANES = 8
LANES = 128
LRU_HALO_PREV = 16
LRU_HALO_NEXT = 8
VMEM_LIMIT_BYTES = 56 * 1024 * 1024

_BF16 = jnp.bfloat16
_F32 = jnp.float32


def _sigmoid(x):
    return 0.5 * jnp.tanh(0.5 * x) + 0.5


def _half_gelu_tanh(x):
    return 0.25 * x * (1.0 + jnp.tanh(0.7978845608028654 * (x + 0.044715 * (x * x * x))))


def _rms_mod(x, g, shift, scale):
    ms = jnp.mean(x * x, axis=-1, keepdims=True)
    return (x * lax.rsqrt(ms + EPS) * g) * (1.0 + scale) + shift


def _mod_slice(m_ref, idx, d):
    return m_ref[:, idx * d:(idx + 1) * d]


def _params(n_grid):
    return pltpu.CompilerParams(dimension_semantics=("arbitrary",) * n_grid,
                                vmem_limit_bytes=VMEM_LIMIT_BYTES)


def _mod_kernel(c_ref, w_ref, b_ref, o_ref):
    c = c_ref[...]
    a = (c * _sigmoid(c)).astype(_BF16)
    o_ref[...] = jnp.dot(a, w_ref[...].astype(_BF16), preferred_element_type=_F32) + b_ref[...]


def _modulation(cc, w_mod, b_mod):
    rows, d = cc.shape
    n = w_mod.shape[1]
    tn = d
    return pl.pallas_call(
        _mod_kernel,
        grid=(n // tn,),
        in_specs=[pl.BlockSpec((rows, d), lambda j: (0, 0)),
                  pl.BlockSpec((d, tn), lambda j: (0, j)),
                  pl.BlockSpec((1, tn), lambda j: (0, j))],
        out_specs=pl.BlockSpec((rows, tn), lambda j: (0, j)),
        out_shape=jax.ShapeDtypeStruct((rows, n), _F32),
        compiler_params=_params(1),
        name="modulation",
    )(cc, w_mod, b_mod.reshape(1, n))


def _ffn_kernel(*refs, d, f, fc, il_in, il_out, add_pos, final_norm, mod_idx):
    it = iter(refs)
    x_ref = next(it)
    prow_ref, pcol_ref = (next(it), next(it)) if add_pos else (None, None)
    m_ref = next(it)
    g_ref = next(it)
    wu_ref = next(it)
    wd_ref = next(it)
    gf_ref = next(it) if final_norm else None
    o_ref = next(it)

    tile = x_ref.shape[0]
    lc = tile // SUBLANES
    x = x_ref[...]
    if il_in:
        x = jnp.swapaxes(x.reshape(lc, SUBLANES, d), 0, 1).reshape(tile, d)
    if add_pos:
        col_half = pcol_ref[...]
        x = x + jnp.concatenate(
            [jnp.concatenate([jnp.broadcast_to(prow_ref[q:q + 1, :], col_half.shape), col_half], axis=-1)
             for q in range(tile // GRID_W)], axis=0)
    shift, scale, gate = (_mod_slice(m_ref, i, d) for i in mod_idx)
    hb = _rms_mod(x, g_ref[...], shift, scale).astype(_BF16)

    acc = None
    for c0 in range(0, f, fc):
        gt = jnp.dot(hb, wu_ref[:, c0:c0 + fc].astype(_BF16), preferred_element_type=_F32)
        up = jnp.dot(hb, wu_ref[:, f + c0:f + c0 + fc].astype(_BF16), preferred_element_type=_F32)
        hg = 0.5 * gt
        act = ((hg + hg * jnp.tanh(hg)) * up).astype(_BF16)
        part = jnp.dot(act, wd_ref[c0:c0 + fc, :].astype(_BF16), preferred_element_type=_F32)
        acc = part if acc is None else acc + part
    y = x + (0.5 * gate) * acc
    if final_norm:
        ms = jnp.mean(y * y, axis=-1, keepdims=True)
        y = y * lax.rsqrt(ms + EPS) * gf_ref[...]
    if il_out:
        y = jnp.swapaxes(y.reshape(SUBLANES, lc, d), 0, 1).reshape(tile, d)
    o_ref[...] = y


def _ffn(x, m3, g, wu, wd, *, tile, il_in, il_out, mod_idx, pos=None, g_final=None, fc=256):
    b, t_len, d = x.shape
    f = wd.shape[0]
    n_t = t_len // tile
    row_block = pl.BlockSpec((None, tile, d), lambda t, i: (i, t, 0))
    in_specs = [row_block]
    args = [x]
    if pos is not None:
        assert tile % GRID_W == 0
        pos_row, pos_col = pos
        in_specs += [pl.BlockSpec((tile // GRID_W, d // 2), lambda t, i: (t, 0)),
                     pl.BlockSpec(pos_col.shape, lambda t, i: (0, 0))]
        args += [pos_row, pos_col]
    in_specs += [pl.BlockSpec((None, 1, N_MOD * d), lambda t, i: (i, 0, 0)),
                 pl.BlockSpec((1, d), lambda t, i: (0, 0)),
                 pl.BlockSpec(wu.shape, lambda t, i: (0, 0), pipeline_mode=pl.Buffered(1)),
                 pl.BlockSpec(wd.shape, lambda t, i: (0, 0), pipeline_mode=pl.Buffered(1))]
    args += [m3, g.reshape(1, d), wu, wd]
    if g_final is not None:
        in_specs.append(pl.BlockSpec((1, d), lambda t, i: (0, 0)))
        args.append(g_final.reshape(1, d))
    kern = functools.partial(_ffn_kernel, d=d, f=f, fc=fc, il_in=il_in, il_out=il_out,
                             add_pos=pos is not None, final_norm=g_final is not None,
                             mod_idx=mod_idx)
    return pl.pallas_call(
        kern, grid=(n_t, b), in_specs=in_specs, out_specs=row_block,
        out_shape=jax.ShapeDtypeStruct((b, t_len, d), _F32),
        compiler_params=_params(2), name="ffn",
    )(*args)


def _inproj_kernel(x_ref, m_ref, g_ref, w_ref, b_ref, *out_refs, d, dc, wl, cw, lru_only, mod_idx):
    shift, scale = (_mod_slice(m_ref, i, d) for i in mod_idx)
    hb = _rms_mod(x_ref[...], g_ref[...], shift, scale).astype(_BF16)

    def proj(c0):
        return jnp.dot(hb, w_ref[:, c0:c0 + cw], preferred_element_type=_F32) + b_ref[:, c0:c0 + cw]

    if lru_only:
        (ux_ref,) = out_refs
        for c0 in range(0, wl, cw):
            ux_ref[:, c0:c0 + cw] = proj(c0)
        return
    u_ref, ux_ref, gl_ref, gt_ref = out_refs
    for c0 in range(0, dc, cw):
        u_ref[:, c0:c0 + cw] = (proj(c0) * _sigmoid(proj(dc + c0))).astype(_BF16)
    for c0 in range(0, wl, cw):
        ux_ref[:, c0:c0 + cw] = proj(2 * dc + c0)
        gl_ref[:, c0:c0 + cw] = _half_gelu_tanh(proj(2 * dc + wl + c0)).astype(_BF16)
    for c0 in range(0, 2 * d, cw):
        gt_ref[:, c0:c0 + cw] = _sigmoid(proj(2 * dc + 2 * wl + c0)).astype(_BF16)


def _inproj(x, m3, g, w, bias, *, tm, dc, wl, lru_only, mod_idx, cw=256):
    b, t, d = x.shape
    n = w.shape[1]
    row = lambda width: pl.BlockSpec((None, tm, width), lambda i, j: (i, j, 0))
    if lru_only:
        out_specs = [row(wl)]
        out_shape = [jax.ShapeDtypeStruct((b, t, wl), _F32)]
    else:
        out_specs = [row(dc), row(wl), row(wl), row(2 * d)]
        out_shape = [jax.ShapeDtypeStruct((b, t, dc), _BF16),
                     jax.ShapeDtypeStruct((b, t, wl), _F32),
                     jax.ShapeDtypeStruct((b, t, wl), _BF16),
                     jax.ShapeDtypeStruct((b, t, 2 * d), _BF16)]
    kern = functools.partial(_inproj_kernel, d=d, dc=dc, wl=wl, cw=cw, lru_only=lru_only,
                             mod_idx=mod_idx)
    return pl.pallas_call(
        kern, grid=(b, t // tm),
        in_specs=[row(d),
                  pl.BlockSpec((None, 1, N_MOD * d), lambda i, j: (i, 0, 0)),
                  pl.BlockSpec((1, d), lambda i, j: (0, 0)),
                  pl.BlockSpec((d, n), lambda i, j: (0, 0)),
                  pl.BlockSpec((1, n), lambda i, j: (0, 0))],
        out_specs=out_specs, out_shape=out_shape,
        compiler_params=_params(2), name="inproj",
    )(x, m3, g.reshape(1, d), w, bias.reshape(1, n))


def _lru_block(xr, wg, bg, lam, carry_ref, ln, reverse):
    lc = xr.shape[0] // SUBLANES
    zh = jnp.dot(xr.astype(_BF16), wg, preferred_element_type=_F32) + bg
    t_rec = jnp.tanh(zh[:, :LANES])
    t_in = jnp.tanh(zh[:, LANES:])
    y = -lam
    softplus = jnp.maximum(y, 0.0) + jnp.log1p(jnp.exp(-jnp.abs(y)))
    half_c = (-0.5 * LRU_C * LOG2_E) * softplus
    a = jnp.exp2(t_rec * half_c + half_c)
    v = 1.0 - a * a
    root = jnp.where(v > 0.0, v * lax.rsqrt(v), 0.0)
    bb = root * ((t_in + 1.0) * xr)

    h = jnp.zeros((SUBLANES, LANES), _F32)
    p = jnp.ones((SUBLANES, LANES), _F32)
    hs, ps = [None] * lc, [None] * lc
    for j in (range(lc - 1, -1, -1) if reverse else range(lc)):
        aj = a[SUBLANES * j:SUBLANES * (j + 1), :]
        h = aj * h + bb[SUBLANES * j:SUBLANES * (j + 1), :]
        p = aj * p
        hs[j], ps[j] = h, p
    sub = lax.broadcasted_iota(jnp.int32, (SUBLANES, LANES), 0)
    hin = carry_ref[:, ln]
    shift = 7 if reverse else 1
    for s in (range(6, -1, -1) if reverse else range(1, SUBLANES)):
        hin = jnp.where(sub == s, pltpu.roll(p * hin + h, shift, 0), hin)
    fin = p * hin + h
    carry_ref[:, ln] = pltpu.roll(fin, shift, 0)
    return jnp.concatenate([hs[j] + ps[j] * hin for j in range(lc)], axis=0), fin


def _lru_kernel(*refs, tile, wl, n_blocks, reverse, combine, has_xr, emit_xr):
    it = iter(refs)
    if has_xr:
        xr_ref = next(it)
    else:
        ux_ref, pv_ref, nx_ref, cw_ref, cb_ref = (next(it) for _ in range(5))
    wg_ref, bg_ref, lam_ref, h0_ref = (next(it) for _ in range(4))
    hf_ref = next(it) if combine else None
    gl_ref = next(it) if combine else None
    out_ref, hlast_ref = next(it), next(it)
    xr_out_ref = next(it) if emit_xr else None
    if not has_xr:
        e_ref = next(it)
    carry_ref = next(it)

    n_t = pl.num_programs(1)
    i = pl.program_id(1)
    tt = n_t - 1 - i if reverse else i
    hp, hn = LRU_HALO_PREV, LRU_HALO_NEXT

    @pl.when(i == 0)
    def _():
        carry_ref[...] = jnp.broadcast_to(h0_ref[...], carry_ref.shape)

    if not has_xr:
        sub_p = lax.broadcasted_iota(jnp.int32, (hp, wl), 0) % SUBLANES
        pv = jnp.where(tt == 0, 0.0, pv_ref[...])
        e_ref[0:hp, :] = jnp.where(sub_p == 0, pltpu.roll(pv, hp - 7, 0),
                                   pltpu.roll(ux_ref[tile - hp:tile, :], 1, 0))
        e_ref[hp:hp + tile, :] = ux_ref[...]
        sub_n = lax.broadcasted_iota(jnp.int32, (hn, wl), 0)
        nx = jnp.where(tt == n_t - 1, 0.0, nx_ref[...])
        e_ref[hp + tile:hp + tile + hn, :] = jnp.where(sub_n == 7, pltpu.roll(nx, 7, 0),
                                                       pltpu.roll(ux_ref[0:hn, :], 7, 0))

    for blk in range(n_blocks):
        ln = slice(blk * LANES, (blk + 1) * LANES)
        if has_xr:
            xr = xr_ref[:, ln]
        else:
            xr = cb_ref[:, ln]
            for k in range(4):
                xr = xr + cw_ref[k:k + 1, ln] * e_ref[SUBLANES * k:SUBLANES * k + tile, ln]
        if emit_xr:
            xr_out_ref[:, ln] = xr
        hfull, fin = _lru_block(xr, wg_ref[blk], bg_ref[blk], lam_ref[:, ln], carry_ref, ln, reverse)
        hlast_ref[:, ln] = fin if reverse else pltpu.roll(fin, 1, 0)
        if combine:
            out_ref[:, ln] = ((hf_ref[:, ln] + hfull) * gl_ref[:, ln].astype(_F32)).astype(_BF16)
        else:
            out_ref[:, ln] = hfull


def _lru(ux, cw, cb, wg, bg, lam, h0, *, tile, reverse, hf=None, gl=None, xr=None, emit_xr=False):
    has_xr = xr is not None
    b, t, wl = (xr if has_xr else ux).shape
    n_t = t // tile
    n_blocks = wl // LANES
    combine = hf is not None
    hp, hn = LRU_HALO_PREV, LRU_HALO_NEXT

    def tmap(j):
        return n_t - 1 - j if reverse else j

    main = lambda width: pl.BlockSpec((None, tile, width), lambda i, j: (i, tmap(j), 0))
    full = lambda a: pl.BlockSpec(a.shape, lambda i, j: (0,) * a.ndim)
    if has_xr:
        in_specs, args = [main(wl)], [xr]
    else:
        prev = pl.BlockSpec((None, hp, wl),
                            lambda i, j: (i, jnp.maximum(tmap(j) * (tile // hp) - 1, 0), 0))
        nxt = pl.BlockSpec((None, hn, wl),
                           lambda i, j: (i, jnp.minimum((tmap(j) + 1) * (tile // hn), t // hn - 1), 0))
        in_specs, args = [main(wl), prev, nxt, full(cw), full(cb)], [ux, ux, ux, cw, cb]
    in_specs += [full(wg), full(bg), full(lam), pl.BlockSpec((None, 1, wl), lambda i, j: (i, 0, 0))]
    args += [wg, bg, lam, h0]
    if combine:
        in_specs += [main(wl), main(wl)]
        args += [hf, gl]
    out_specs = [main(wl), pl.BlockSpec((None, SUBLANES, wl), lambda i, j: (i, 0, 0))]
    out_shape = [jax.ShapeDtypeStruct((b, t, wl), _BF16 if combine else _F32),
                 jax.ShapeDtypeStruct((b, SUBLANES, wl), _F32)]
    if emit_xr:
        out_specs.append(main(wl))
        out_shape.append(jax.ShapeDtypeStruct((b, t, wl), _F32))
    scratch = [] if has_xr else [pltpu.VMEM((hp + tile + hn, wl), _F32)]
    scratch.append(pltpu.VMEM((SUBLANES, wl), _F32))
    kern = functools.partial(_lru_kernel, tile=tile, wl=wl, n_blocks=n_blocks, reverse=reverse,
                             combine=combine, has_xr=has_xr, emit_xr=emit_xr)
    return pl.pallas_call(
        kern, grid=(b, n_t), in_specs=in_specs, out_specs=out_specs, out_shape=out_shape,
        scratch_shapes=scratch, compiler_params=_params(2),
        name="lru_bwd" if reverse else "lru_fwd",
    )(*args)


def _mix_kernel(u_ref, up_ref, un_ref, yl_ref, gt_ref, x_ref, m_ref, wdw_ref, bdw_ref,
                gln_ref, bln_ref, wco_ref, wlo_ref, wo_ref, o_ref, e_ref, yc_ref,
                *, tile, d, taps, halo, group, gate_idx):
    n_t = pl.num_programs(1)
    i = pl.program_id(1)
    pad = taps // 2
    lc = tile // SUBLANES
    hb_ = halo // SUBLANES

    sub = lax.broadcasted_iota(jnp.int32, (halo, d), 0) % SUBLANES
    pv = jnp.where(i == 0, 0.0, up_ref[...].astype(_F32))
    e_ref[0:halo, :] = jnp.where(sub == 0, pltpu.roll(pv, halo - 7, 0),
                                 pltpu.roll(u_ref[tile - halo:tile, :].astype(_F32), 1, 0))
    e_ref[halo:halo + tile, :] = u_ref[...].astype(_F32)
    nx = jnp.where(i == n_t - 1, 0.0, un_ref[...].astype(_F32))
    e_ref[halo + tile:2 * halo + tile, :] = jnp.where(
        sub == 7, pltpu.roll(nx, 7, 0), pltpu.roll(u_ref[0:halo, :].astype(_F32), halo - 1, 0))

    rows = group * SUBLANES
    for lg in range(d // LANES):
        ln = slice(lg * LANES, (lg + 1) * LANES)
        w = [jnp.broadcast_to(wdw_ref[k:k + 1, ln], (SUBLANES, LANES)) for k in range(taps)]
        bias = jnp.broadcast_to(bdw_ref[:, ln], (SUBLANES, LANES))

        def conv_group(gi, carry, ln=ln, w=w, bias=bias):
            r0 = pl.multiple_of(gi * rows, rows)
            acc = [[bias] * group, [None] * group]
            for m in range(group + taps - 1):
                blk = e_ref[pl.ds(r0 + SUBLANES * (hb_ - pad + m), SUBLANES), ln]
                for j in range(max(0, m - taps + 1), min(group, m + 1)):
                    part, term = acc[(m - j) % 2], w[m - j] * blk
                    part[j] = term if part[j] is None else part[j] + term
            yc_ref[pl.ds(r0, rows), ln] = jnp.concatenate(
                [acc[0][j] + acc[1][j] for j in range(group)], axis=0)
            return carry

        lax.fori_loop(0, lc // group, conv_group, 0)

    yc = yc_ref[...]
    mu = jnp.mean(yc, axis=-1, keepdims=True)
    xc = yc - mu
    ln_out = xc * lax.rsqrt(jnp.mean(xc * xc, axis=-1, keepdims=True) + EPS) * gln_ref[...] + bln_ref[...]
    act = (ln_out * _sigmoid(ln_out)).astype(_BF16)
    y_conf = jnp.dot(act, wco_ref[...], preferred_element_type=_F32)
    y_lru = jnp.dot(yl_ref[...], wlo_ref[...], preferred_element_type=_F32)
    mixed = (gt_ref[:, :d].astype(_F32) * y_conf + gt_ref[:, d:].astype(_F32) * y_lru).astype(_BF16)
    y = jnp.dot(mixed, wo_ref[...], preferred_element_type=_F32)
    o_ref[...] = x_ref[...] + _mod_slice(m_ref, gate_idx, d) * y


def _mix(u, yl, gates, x1, m3, w_dw, b_dw, g_ln, b_ln, wco, wlo, wo, *, tile, gate_idx,
         halo=128, group=8):
    b, t, d = x1.shape
    wl = yl.shape[-1]
    taps = w_dw.shape[0]
    n_t = t // tile
    assert halo // SUBLANES >= taps // 2 + 1 and tile >= halo and (tile // SUBLANES) % group == 0
    main = lambda width: pl.BlockSpec((None, tile, width), lambda i, j: (i, j, 0))
    prev = pl.BlockSpec((None, halo, d), lambda i, j: (i, jnp.maximum(j * (tile // halo) - 1, 0), 0))
    nxt = pl.BlockSpec((None, halo, d),
                       lambda i, j: (i, jnp.minimum((j + 1) * (tile // halo), t // halo - 1), 0))
    full = lambda a: pl.BlockSpec(a.shape, lambda i, j: (0,) * a.ndim)
    vec = lambda a: a.reshape(1, -1)
    consts = [w_dw, vec(b_dw), vec(g_ln), vec(b_ln), wco, wlo, wo]
    kern = functools.partial(_mix_kernel, tile=tile, d=d, taps=taps, halo=halo, group=group,
                             gate_idx=gate_idx)
    return pl.pallas_call(
        kern, grid=(b, n_t),
        in_specs=[main(d), prev, nxt, main(wl), main(2 * d), main(d),
                  pl.BlockSpec((None, 1, N_MOD * d), lambda i, j: (i, 0, 0))]
                 + [full(a) for a in consts],
        out_specs=main(d),
        out_shape=jax.ShapeDtypeStruct((b, t, d), _F32),
        scratch_shapes=[pltpu.VMEM((tile + 2 * halo, d), _F32), pltpu.VMEM((tile, d), _F32)],
        compiler_params=_params(2), name="mix",
    )(u, u, u, yl, gates, x1, m3, *consts)


def _grid_pos_tables(seq_len, dim):
    q = dim // 4
    omega = 1.0 / (10000.0 ** (jnp.arange(q, dtype=_F32) / q))
    er = jnp.arange(seq_len // GRID_W).astype(_F32)[:, None] * omega
    ec = jnp.arange(GRID_W).astype(_F32)[:, None] * omega
    return (jnp.concatenate([jnp.sin(er), jnp.cos(er)], axis=-1),
            jnp.concatenate([jnp.sin(ec), jnp.cos(ec)], axis=-1))


def _gate_weights(w_rec, b_rec, w_in, b_in, direction):
    n_blocks, bw, _ = w_rec.shape[1:]
    wg = (0.5 * jnp.concatenate([w_rec[direction], w_in[direction]], axis=-1)).astype(_BF16)
    bg = 0.5 * jnp.concatenate([b_rec[direction].reshape(n_blocks, 1, bw),
                                b_in[direction].reshape(n_blocks, 1, bw)], axis=-1)
    return wg, bg


def _layer(x, c, ctx, c_ctx, lp, g_final, *, tile, ctx_tile):
    b, t, d = x.shape
    dc = lp["w_dw"].shape[-1]
    wl = lp["w_lru_conv"].shape[-1]
    col_lru = 2 * dc

    cc = jnp.concatenate([c, c_ctx[None, :]], axis=0)
    cc = jnp.pad(cc, ((0, SUBLANES - cc.shape[0] % SUBLANES), (0, 0)))
    m_all = _modulation(cc, lp["w_mod"], lp["b_mod"])
    m3 = m_all[:b, None, :]
    mc3 = jnp.broadcast_to(m_all[b][None, None, :], (b, 1, N_MOD * d))

    wu1, wd1 = lp["w_ffn1_up"], lp["w_ffn1_down"]
    wu2, wd2 = lp["w_ffn2_up"], lp["w_ffn2_down"]
    w_in = lp["w_in"].astype(_BF16)
    lam = lp["lru_lambda"]
    gates = [_gate_weights(lp["w_rec_gate"], lp["b_rec_gate"], lp["w_in_gate"], lp["b_in_gate"], k)
             for k in range(2)]
    cw, cb = lp["w_lru_conv"], lp["b_lru_conv"].reshape(1, wl)

    def lru_pair(ux, h0f, h0b, tl, hf_gl=None):
        hf, hf_last, xr = _lru(ux, cw, cb, gates[0][0], gates[0][1], lam[0:1], h0f, tile=tl,
                               reverse=False, emit_xr=True)
        extra = {} if hf_gl is None else dict(hf=hf, gl=hf_gl)
        out, hb_last = _lru(None, None, None, gates[1][0], gates[1][1], lam[1:2], h0b, tile=tl,
                            reverse=True, xr=xr, **extra)
        return out, hf_last[:, 0:1, :], hb_last[:, 0:1, :]

    xc1 = _ffn(ctx, mc3, lp["g_n1"], wu1, wd1, tile=ctx_tile, il_in=False, il_out=True,
               mod_idx=(0, 1, 2))
    (uxc,) = _inproj(xc1, mc3, lp["g_n2"], w_in[:, col_lru:col_lru + wl],
                     lp["b_in"][col_lru:col_lru + wl], tm=ctx_tile, dc=dc, wl=wl,
                     lru_only=True, mod_idx=(3, 4))
    zeros = jnp.zeros((b, 1, wl), _F32)
    _, h0f, h0b = lru_pair(uxc, zeros, zeros, ctx_tile)

    pos = _grid_pos_tables(t, d)
    x1 = _ffn(x, m3, lp["g_n1"], wu1, wd1, tile=tile, il_in=False, il_out=True,
              mod_idx=(0, 1, 2), pos=pos)
    u, ux, gl, gts = _inproj(x1, m3, lp["g_n2"], w_in, lp["b_in"], tm=tile, dc=dc, wl=wl,
                             lru_only=False, mod_idx=(3, 4))
    yl, _, _ = lru_pair(ux, h0f, h0b, tile, hf_gl=gl)
    x2 = _mix(u, yl, gts, x1, m3, lp["w_dw"], lp["b_dw"], lp["g_ln"], lp["b_ln"],
              lp["w_conf_out"].astype(_BF16), lp["w_lru_out"].astype(_BF16),
              lp["w_out"].astype(_BF16), tile=tile, gate_idx=5)
    return _ffn(x2, m3, lp["g_n3"], wu2, wd2, tile=tile, il_in=True, il_out=False,
                mod_idx=(6, 7, 8), g_final=g_final)


def _forward(x, c, ctx, c_ctx, params, g_final, *, tile=512, ctx_tile=256):
    depth = params["w_mod"].shape[0]
    assert depth == 1, "only the single-layer (context read-only) block is implemented"
    lp = {k: v[0] for k, v in params.items()}
    return _layer(x, c, ctx, c_ctx, lp, g_final, tile=tile, ctx_tile=ctx_tile)


def kernel(x, c, ctx, c_ctx, w_mod, b_mod, g_n1, w_ffn1_up, w_ffn1_down, g_n2, w_in, b_in, w_dw, b_dw, g_ln, b_ln, w_conf_out, w_lru_conv, b_lru_conv, w_rec_gate, b_rec_gate, w_in_gate, b_in_gate, lru_lambda, w_lru_out, w_out, g_n3, w_ffn2_up, w_ffn2_down, g_final):
    params = dict(w_mod=w_mod, b_mod=b_mod, g_n1=g_n1, w_ffn1_up=w_ffn1_up, w_ffn1_down=w_ffn1_down,
                  g_n2=g_n2, w_in=w_in, b_in=b_in, w_dw=w_dw, b_dw=b_dw, g_ln=g_ln, b_ln=b_ln,
                  w_conf_out=w_conf_out, w_lru_conv=w_lru_conv, b_lru_conv=b_lru_conv,
                  w_rec_gate=w_rec_gate, b_rec_gate=b_rec_gate, w_in_gate=w_in_gate,
                  b_in_gate=b_in_gate, lru_lambda=lru_lambda, w_lru_out=w_lru_out, w_out=w_out,
                  g_n3=g_n3, w_ffn2_up=w_ffn2_up, w_ffn2_down=w_ffn2_down)
    return _forward(x, c, ctx, c_ctx, params, g_final)
```

```python
import functools

import jax
import jax.numpy as jnp
from jax import lax
from jax.experimental import pallas as pl
from jax.experimental.pallas import tpu as pltpu

EPS = 1e-6
LRU_C = 8.0
LOG2_E = 1.4426950408889634
GRID_W = 64
N_MOD = 9
SUBLANES = 8
LANES = 128
LRU_HALO_PREV = 16
LRU_HALO_NEXT = 8
VMEM_LIMIT_BYTES = 56 * 1024 * 1024

_BF16 = jnp.bfloat16
_F32 = jnp.float32


def _sigmoid(x):
    return 0.5 * jnp.tanh(0.5 * x) + 0.5


def _half_gelu_tanh(x):
    return 0.25 * x * (1.0 + jnp.tanh(0.7978845608028654 * (x + 0.044715 * (x * x * x))))


def _rms_mod(x, g, shift, scale):
    ms = jnp.mean(x * x, axis=-1, keepdims=True)
    return (x * lax.rsqrt(ms + EPS)) * (g * (1.0 + scale)) + shift


def _mod_slice(m_ref, idx, d):
    return m_ref[:, idx * d:(idx + 1) * d]


def _params(n_grid):
    return pltpu.CompilerParams(dimension_semantics=("arbitrary",) * n_grid,
                                vmem_limit_bytes=VMEM_LIMIT_BYTES)


def _mod_kernel(c_ref, w_ref, b_ref, o_ref):
    c = c_ref[...]
    a = (c * _sigmoid(c)).astype(_BF16)
    o_ref[...] = jnp.dot(a, w_ref[...].astype(_BF16), preferred_element_type=_F32) + b_ref[...]


def _modulation(cc, w_mod, b_mod):
    rows, d = cc.shape
    n = w_mod.shape[1]
    tn = d
    return pl.pallas_call(
        _mod_kernel,
        grid=(n // tn,),
        in_specs=[pl.BlockSpec((rows, d), lambda j: (0, 0)),
                  pl.BlockSpec((d, tn), lambda j: (0, j)),
                  pl.BlockSpec((1, tn), lambda j: (0, j))],
        out_specs=pl.BlockSpec((rows, tn), lambda j: (0, j)),
        out_shape=jax.ShapeDtypeStruct((rows, n), _F32),
        compiler_params=_params(1),
        name="modulation",
    )(cc, w_mod, b_mod.reshape(1, n))


def _ffn_kernel(*refs, d, f, fc, il_in, il_out, add_pos, final_norm, mod_idx):
    it = iter(refs)
    x_ref = next(it)
    prow_ref, pcol_ref = (next(it), next(it)) if add_pos else (None, None)
    m_ref = next(it)
    g_ref = next(it)
    wu_ref = next(it)
    wd_ref = next(it)
    gf_ref = next(it) if final_norm else None
    o_ref = next(it)

    tile = x_ref.shape[0]
    lc = tile // SUBLANES
    x = x_ref[...]
    if il_in:
        x = jnp.swapaxes(x.reshape(lc, SUBLANES, d), 0, 1).reshape(tile, d)
    if add_pos:
        col_half = pcol_ref[...]
        x = x + jnp.concatenate(
            [jnp.concatenate([jnp.broadcast_to(prow_ref[q:q + 1, :], col_half.shape), col_half], axis=-1)
             for q in range(tile // GRID_W)], axis=0)
    shift, scale, gate = (_mod_slice(m_ref, i, d) for i in mod_idx)
    hb = _rms_mod(x, g_ref[...], shift, scale).astype(_BF16)

    acc = None
    for c0 in range(0, f, fc):
        gt = jnp.dot(hb, wu_ref[:, c0:c0 + fc].astype(_BF16), preferred_element_type=_F32)
        up = jnp.dot(hb, wu_ref[:, f + c0:f + c0 + fc].astype(_BF16), preferred_element_type=_F32)
        hg = 0.5 * gt
        act = ((hg + hg * jnp.tanh(hg)) * up).astype(_BF16)
        part = jnp.dot(act, wd_ref[c0:c0 + fc, :].astype(_BF16), preferred_element_type=_F32)
        acc = part if acc is None else acc + part
    y = x + (0.5 * gate) * acc
    if final_norm:
        ms = jnp.mean(y * y, axis=-1, keepdims=True)
        y = y * lax.rsqrt(ms + EPS) * gf_ref[...]
    if il_out:
        y = jnp.swapaxes(y.reshape(SUBLANES, lc, d), 0, 1).reshape(tile, d)
    o_ref[...] = y


def _ffn(x, m3, g, wu, wd, *, tile, il_in, il_out, mod_idx, pos=None, g_final=None, fc=256):
    b, t_len, d = x.shape
    f = wd.shape[0]
    n_t = t_len // tile
    row_block = pl.BlockSpec((None, tile, d), lambda t, i: (i, t, 0))
    in_specs = [row_block]
    args = [x]
    if pos is not None:
        assert tile % GRID_W == 0
        pos_row, pos_col = pos
        in_specs += [pl.BlockSpec((tile // GRID_W, d // 2), lambda t, i: (t, 0)),
                     pl.BlockSpec(pos_col.shape, lambda t, i: (0, 0))]
        args += [pos_row, pos_col]
    in_specs += [pl.BlockSpec((None, 1, N_MOD * d), lambda t, i: (i, 0, 0)),
                 pl.BlockSpec((1, d), lambda t, i: (0, 0)),
                 pl.BlockSpec(wu.shape, lambda t, i: (0, 0), pipeline_mode=pl.Buffered(1)),
                 pl.BlockSpec(wd.shape, lambda t, i: (0, 0), pipeline_mode=pl.Buffered(1))]
    args += [m3, g.reshape(1, d), wu, wd]
    if g_final is not None:
        in_specs.append(pl.BlockSpec((1, d), lambda t, i: (0, 0)))
        args.append(g_final.reshape(1, d))
    kern = functools.partial(_ffn_kernel, d=d, f=f, fc=fc, il_in=il_in, il_out=il_out,
                             add_pos=pos is not None, final_norm=g_final is not None,
                             mod_idx=mod_idx)
    return pl.pallas_call(
        kern, grid=(n_t, b), in_specs=in_specs, out_specs=row_block,
        out_shape=jax.ShapeDtypeStruct((b, t_len, d), _F32),
        compiler_params=_params(2), name="ffn",
    )(*args)


def _inproj_kernel(x_ref, m_ref, g_ref, w_ref, b_ref, *out_refs, d, dc, wl, cw, lru_only, mod_idx):
    shift, scale = (_mod_slice(m_ref, i, d) for i in mod_idx)
    hb = _rms_mod(x_ref[...], g_ref[...], shift, scale).astype(_BF16)

    def proj(c0):
        return jnp.dot(hb, w_ref[:, c0:c0 + cw], preferred_element_type=_F32) + b_ref[:, c0:c0 + cw]

    if lru_only:
        (ux_ref,) = out_refs
        for c0 in range(0, wl, cw):
            ux_ref[:, c0:c0 + cw] = proj(c0)
        return
    u_ref, ux_ref, gl_ref, gt_ref = out_refs
    for c0 in range(0, dc, cw):
        u_ref[:, c0:c0 + cw] = (proj(c0) * _sigmoid(proj(dc + c0))).astype(_BF16)
    for c0 in range(0, wl, cw):
        ux_ref[:, c0:c0 + cw] = proj(2 * dc + c0)
        gl_ref[:, c0:c0 + cw] = _half_gelu_tanh(proj(2 * dc + wl + c0)).astype(_BF16)
    for c0 in range(0, 2 * d, cw):
        gt_ref[:, c0:c0 + cw] = _sigmoid(proj(2 * dc + 2 * wl + c0)).astype(_BF16)


def _inproj(x, m3, g, w, bias, *, tm, dc, wl, lru_only, mod_idx, cw=256):
    b, t, d = x.shape
    n = w.shape[1]
    row = lambda width: pl.BlockSpec((None, tm, width), lambda i, j: (i, j, 0))
    if lru_only:
        out_specs = [row(wl)]
        out_shape = [jax.ShapeDtypeStruct((b, t, wl), _F32)]
    else:
        out_specs = [row(dc), row(wl), row(wl), row(2 * d)]
        out_shape = [jax.ShapeDtypeStruct((b, t, dc), _BF16),
                     jax.ShapeDtypeStruct((b, t, wl), _F32),
                     jax.ShapeDtypeStruct((b, t, wl), _BF16),
                     jax.ShapeDtypeStruct((b, t, 2 * d), _BF16)]
    kern = functools.partial(_inproj_kernel, d=d, dc=dc, wl=wl, cw=cw, lru_only=lru_only,
                             mod_idx=mod_idx)
    return pl.pallas_call(
        kern, grid=(b, t // tm),
        in_specs=[row(d),
                  pl.BlockSpec((None, 1, N_MOD * d), lambda i, j: (i, 0, 0)),
                  pl.BlockSpec((1, d), lambda i, j: (0, 0)),
                  pl.BlockSpec((d, n), lambda i, j: (0, 0)),
                  pl.BlockSpec((1, n), lambda i, j: (0, 0))],
        out_specs=out_specs, out_shape=out_shape,
        compiler_params=_params(2), name="inproj",
    )(x, m3, g.reshape(1, d), w, bias.reshape(1, n))


def _lru_block(xr, wg, bg, lam, carry_ref, ln, reverse):
    lc = xr.shape[0] // SUBLANES
    zh = jnp.dot(xr.astype(_BF16), wg, preferred_element_type=_F32) + bg
    t_rec = jnp.tanh(zh[:, :LANES])
    t_in = jnp.tanh(zh[:, LANES:])
    y = -lam
    softplus = jnp.maximum(y, 0.0) + jnp.log1p(jnp.exp(-jnp.abs(y)))
    half_c = (-0.5 * LRU_C * LOG2_E) * softplus
    a = jnp.exp2((t_rec + 1.0) * half_c)
    v = 1.0 - a * a
    root = jnp.where(v > 0.0, v * lax.rsqrt(v), 0.0)
    bb = root * ((t_in + 1.0) * xr)

    order = range(lc - 1, -1, -1) if reverse else range(lc)
    h = jnp.zeros((SUBLANES, LANES), _F32)
    p = jnp.ones((SUBLANES, LANES), _F32)
    for j in order:
        aj = a[SUBLANES * j:SUBLANES * (j + 1), :]
        h = aj * h + bb[SUBLANES * j:SUBLANES * (j + 1), :]
        p = aj * p
    sub = lax.broadcasted_iota(jnp.int32, (SUBLANES, LANES), 0)
    hin = carry_ref[:, ln]
    shift = 7 if reverse else 1
    for s in (range(6, -1, -1) if reverse else range(1, SUBLANES)):
        hin = jnp.where(sub == s, pltpu.roll(p * hin + h, shift, 0), hin)
    fin = p * hin + h
    carry_ref[:, ln] = pltpu.roll(fin, shift, 0)
    hs = [None] * lc
    h = hin
    for j in order:
        h = a[SUBLANES * j:SUBLANES * (j + 1), :] * h + bb[SUBLANES * j:SUBLANES * (j + 1), :]
        hs[j] = h
    return jnp.concatenate(hs, axis=0), fin


def _lru_kernel(*refs, tile, wl, n_blocks, reverse, combine, has_xr, emit_xr):
    it = iter(refs)
    if has_xr:
        xr_ref = next(it)
    else:
        ux_ref, pv_ref, nx_ref, cw_ref, cb_ref = (next(it) for _ in range(5))
    wg_ref, bg_ref, lam_ref, h0_ref = (next(it) for _ in range(4))
    hf_ref = next(it) if combine else None
    gl_ref = next(it) if combine else None
    out_ref, hlast_ref = next(it), next(it)
    xr_out_ref = next(it) if emit_xr else None
    if not has_xr:
        e_ref = next(it)
    carry_ref = next(it)

    n_t = pl.num_programs(1)
    i = pl.program_id(1)
    tt = n_t - 1 - i if reverse else i
    hp, hn = LRU_HALO_PREV, LRU_HALO_NEXT

    @pl.when(i == 0)
    def _():
        carry_ref[...] = jnp.broadcast_to(h0_ref[...], carry_ref.shape)

    if not has_xr:
        sub_p = lax.broadcasted_iota(jnp.int32, (hp, wl), 0) % SUBLANES
        pv = jnp.where(tt == 0, 0.0, pv_ref[...])
        e_ref[0:hp, :] = jnp.where(sub_p == 0, pltpu.roll(pv, hp - 7, 0),
                                   pltpu.roll(ux_ref[tile - hp:tile, :], 1, 0))
        e_ref[hp:hp + tile, :] = ux_ref[...]
        sub_n = lax.broadcasted_iota(jnp.int32, (hn, wl), 0)
        nx = jnp.where(tt == n_t - 1, 0.0, nx_ref[...])
        e_ref[hp + tile:hp + tile + hn, :] = jnp.where(sub_n == 7, pltpu.roll(nx, 7, 0),
                                                       pltpu.roll(ux_ref[0:hn, :], 7, 0))

    for blk in range(n_blocks):
        ln = slice(blk * LANES, (blk + 1) * LANES)
        if has_xr:
            xr = xr_ref[:, ln]
        else:
            xr = cb_ref[:, ln]
            for k in range(4):
                xr = xr + cw_ref[k:k + 1, ln] * e_ref[SUBLANES * k:SUBLANES * k + tile, ln]
        if emit_xr:
            xr_out_ref[:, ln] = xr
        hfull, fin = _lru_block(xr, wg_ref[blk], bg_ref[blk], lam_ref[:, ln], carry_ref, ln, reverse)
        hlast_ref[:, ln] = fin if reverse else pltpu.roll(fin, 1, 0)
        if combine:
            out_ref[:, ln] = ((hf_ref[:, ln] + hfull) * gl_ref[:, ln].astype(_F32)).astype(_BF16)
        else:
            out_ref[:, ln] = hfull


def _lru(ux, cw, cb, wg, bg, lam, h0, *, tile, reverse, hf=None, gl=None, xr=None, emit_xr=False):
    has_xr = xr is not None
    b, t, wl = (xr if has_xr else ux).shape
    n_t = t // tile
    n_blocks = wl // LANES
    combine = hf is not None
    hp, hn = LRU_HALO_PREV, LRU_HALO_NEXT

    def tmap(j):
        return n_t - 1 - j if reverse else j

    main = lambda width: pl.BlockSpec((None, tile, width), lambda i, j: (i, tmap(j), 0))
    full = lambda a: pl.BlockSpec(a.shape, lambda i, j: (0,) * a.ndim)
    if has_xr:
        in_specs, args = [main(wl)], [xr]
    else:
        prev = pl.BlockSpec((None, hp, wl),
                            lambda i, j: (i, jnp.maximum(tmap(j) * (tile // hp) - 1, 0), 0))
        nxt = pl.BlockSpec((None, hn, wl),
                           lambda i, j: (i, jnp.minimum((tmap(j) + 1) * (tile // hn), t // hn - 1), 0))
        in_specs, args = [main(wl), prev, nxt, full(cw), full(cb)], [ux, ux, ux, cw, cb]
    in_specs += [full(wg), full(bg), full(lam), pl.BlockSpec((None, 1, wl), lambda i, j: (i, 0, 0))]
    args += [wg, bg, lam, h0]
    if combine:
        in_specs += [main(wl), main(wl)]
        args += [hf, gl]
    out_specs = [main(wl), pl.BlockSpec((None, SUBLANES, wl), lambda i, j: (i, 0, 0))]
    out_shape = [jax.ShapeDtypeStruct((b, t, wl), _BF16 if combine else _F32),
                 jax.ShapeDtypeStruct((b, SUBLANES, wl), _F32)]
    if emit_xr:
        out_specs.append(main(wl))
        out_shape.append(jax.ShapeDtypeStruct((b, t, wl), _F32))
    scratch = [] if has_xr else [pltpu.VMEM((hp + tile + hn, wl), _F32)]
    scratch.append(pltpu.VMEM((SUBLANES, wl), _F32))
    kern = functools.partial(_lru_kernel, tile=tile, wl=wl, n_blocks=n_blocks, reverse=reverse,
                             combine=combine, has_xr=has_xr, emit_xr=emit_xr)
    return pl.pallas_call(
        kern, grid=(b, n_t), in_specs=in_specs, out_specs=out_specs, out_shape=out_shape,
        scratch_shapes=scratch, compiler_params=_params(2),
        name="lru_bwd" if reverse else "lru_fwd",
    )(*args)


def _mix_kernel(u_ref, up_ref, un_ref, yl_ref, gt_ref, x_ref, m_ref, wdw_ref, bdw_ref,
                gln_ref, bln_ref, wco_ref, wlo_ref, wo_ref, o_ref, e_ref, yc_ref,
                *, tile, d, taps, halo, group, gate_idx):
    n_t = pl.num_programs(1)
    i = pl.program_id(1)
    pad = taps // 2
    lc = tile // SUBLANES
    hb_ = halo // SUBLANES

    sub = lax.broadcasted_iota(jnp.int32, (halo, d), 0) % SUBLANES
    pv = jnp.where(i == 0, 0.0, up_ref[...].astype(_F32))
    e_ref[0:halo, :] = jnp.where(sub == 0, pltpu.roll(pv, halo - 7, 0),
                                 pltpu.roll(u_ref[tile - halo:tile, :].astype(_F32), 1, 0))
    e_ref[halo:halo + tile, :] = u_ref[...].astype(_F32)
    nx = jnp.where(i == n_t - 1, 0.0, un_ref[...].astype(_F32))
    e_ref[halo + tile:2 * halo + tile, :] = jnp.where(
        sub == 7, pltpu.roll(nx, 7, 0), pltpu.roll(u_ref[0:halo, :].astype(_F32), halo - 1, 0))

    rows = group * SUBLANES
    for lg in range(d // LANES):
        ln = slice(lg * LANES, (lg + 1) * LANES)
        bias = jnp.broadcast_to(bdw_ref[:, ln], (SUBLANES, LANES))

        def conv_group(gi, carry, ln=ln, bias=bias):
            r0 = pl.multiple_of(gi * rows, rows)
            acc = [[bias] * group] + [[None] * group for _ in range(3)]
            for m in range(group + taps - 1):
                blk = e_ref[pl.ds(r0 + SUBLANES * (hb_ - pad + m), SUBLANES), ln]
                for j in range(max(0, m - taps + 1), min(group, m + 1)):
                    part, term = acc[(m - j) % 4], wdw_ref[m - j:m - j + 1, ln] * blk
                    part[j] = term if part[j] is None else part[j] + term
            yc_ref[pl.ds(r0, rows), ln] = jnp.concatenate(
                [(acc[0][j] + acc[1][j]) + (acc[2][j] + acc[3][j]) for j in range(group)], axis=0)
            return carry

        lax.fori_loop(0, lc // group, conv_group, 0)

    yc = yc_ref[...]
    mu = jnp.mean(yc, axis=-1, keepdims=True)
    xc = yc - mu
    ln_out = xc * lax.rsqrt(jnp.mean(xc * xc, axis=-1, keepdims=True) + EPS) * gln_ref[...] + bln_ref[...]
    half = 0.5 * ln_out
    act = (half + half * jnp.tanh(half)).astype(_BF16)
    y_conf = jnp.dot(act, wco_ref[...], preferred_element_type=_F32)
    y_lru = jnp.dot(yl_ref[...], wlo_ref[...], preferred_element_type=_F32)
    mixed = (gt_ref[:, :d].astype(_F32) * y_conf + gt_ref[:, d:].astype(_F32) * y_lru).astype(_BF16)
    y = jnp.dot(mixed, wo_ref[...], preferred_element_type=_F32)
    o_ref[...] = x_ref[...] + _mod_slice(m_ref, gate_idx, d) * y


def _mix(u, yl, gates, x1, m3, w_dw, b_dw, g_ln, b_ln, wco, wlo, wo, *, tile, gate_idx,
         halo=128, group=8):
    b, t, d = x1.shape
    wl = yl.shape[-1]
    taps = w_dw.shape[0]
    n_t = t // tile
    assert halo // SUBLANES >= taps // 2 + 1 and tile >= halo and (tile // SUBLANES) % group == 0
    main = lambda width: pl.BlockSpec((None, tile, width), lambda i, j: (i, j, 0))
    prev = pl.BlockSpec((None, halo, d), lambda i, j: (i, jnp.maximum(j * (tile // halo) - 1, 0), 0))
    nxt = pl.BlockSpec((None, halo, d),
                       lambda i, j: (i, jnp.minimum((j + 1) * (tile // halo), t // halo - 1), 0))
    full = lambda a: pl.BlockSpec(a.shape, lambda i, j: (0,) * a.ndim)
    vec = lambda a: a.reshape(1, -1)
    consts = [w_dw, vec(b_dw), vec(g_ln), vec(b_ln), wco, wlo, wo]
    kern = functools.partial(_mix_kernel, tile=tile, d=d, taps=taps, halo=halo, group=group,
                             gate_idx=gate_idx)
    return pl.pallas_call(
        kern, grid=(b, n_t),
        in_specs=[main(d), prev, nxt, main(wl), main(2 * d), main(d),
                  pl.BlockSpec((None, 1, N_MOD * d), lambda i, j: (i, 0, 0))]
                 + [full(a) for a in consts],
        out_specs=main(d),
        out_shape=jax.ShapeDtypeStruct((b, t, d), _F32),
        scratch_shapes=[pltpu.VMEM((tile + 2 * halo, d), _F32), pltpu.VMEM((tile, d), _F32)],
        compiler_params=_params(2), name="mix",
    )(u, u, u, yl, gates, x1, m3, *consts)


def _grid_pos_tables(seq_len, dim):
    q = dim // 4
    omega = 1.0 / (10000.0 ** (jnp.arange(q, dtype=_F32) / q))
    er = jnp.arange(seq_len // GRID_W).astype(_F32)[:, None] * omega
    ec = jnp.arange(GRID_W).astype(_F32)[:, None] * omega
    return (jnp.concatenate([jnp.sin(er), jnp.cos(er)], axis=-1),
            jnp.concatenate([jnp.sin(ec), jnp.cos(ec)], axis=-1))


def _gate_weights(w_rec, b_rec, w_in, b_in, direction):
    n_blocks, bw, _ = w_rec.shape[1:]
    wg = (0.5 * jnp.concatenate([w_rec[direction], w_in[direction]], axis=-1)).astype(_BF16)
    bg = 0.5 * jnp.concatenate([b_rec[direction].reshape(n_blocks, 1, bw),
                                b_in[direction].reshape(n_blocks, 1, bw)], axis=-1)
    return wg, bg


def _layer(x, c, ctx, c_ctx, lp, g_final, *, tile, ctx_tile):
    b, t, d = x.shape
    dc = lp["w_dw"].shape[-1]
    wl = lp["w_lru_conv"].shape[-1]
    col_lru = 2 * dc

    cc = jnp.concatenate([c, c_ctx[None, :]], axis=0)
    cc = jnp.pad(cc, ((0, SUBLANES - cc.shape[0] % SUBLANES), (0, 0)))
    m_all = _modulation(cc, lp["w_mod"], lp["b_mod"])
    m3 = m_all[:b, None, :]
    mc3 = jnp.broadcast_to(m_all[b][None, None, :], (b, 1, N_MOD * d))

    wu1, wd1 = lp["w_ffn1_up"], lp["w_ffn1_down"]
    wu2, wd2 = lp["w_ffn2_up"], lp["w_ffn2_down"]
    w_in = lp["w_in"].astype(_BF16)
    lam = lp["lru_lambda"]
    gates = [_gate_weights(lp["w_rec_gate"], lp["b_rec_gate"], lp["w_in_gate"], lp["b_in_gate"], k)
             for k in range(2)]
    cw, cb = lp["w_lru_conv"], lp["b_lru_conv"].reshape(1, wl)

    def lru_pair(ux, h0f, h0b, tl, hf_gl=None):
        hf, hf_last, xr = _lru(ux, cw, cb, gates[0][0], gates[0][1], lam[0:1], h0f, tile=tl,
                               reverse=False, emit_xr=True)
        extra = {} if hf_gl is None else dict(hf=hf, gl=hf_gl)
        out, hb_last = _lru(None, None, None, gates[1][0], gates[1][1], lam[1:2], h0b, tile=tl,
                            reverse=True, xr=xr, **extra)
        return out, hf_last[:, 0:1, :], hb_last[:, 0:1, :]

    xc1 = _ffn(ctx, mc3, lp["g_n1"], wu1, wd1, tile=ctx_tile, il_in=False, il_out=True,
               mod_idx=(0, 1, 2))
    (uxc,) = _inproj(xc1, mc3, lp["g_n2"], w_in[:, col_lru:col_lru + wl],
                     lp["b_in"][col_lru:col_lru + wl], tm=ctx_tile, dc=dc, wl=wl,
                     lru_only=True, mod_idx=(3, 4))
    zeros = jnp.zeros((b, 1, wl), _F32)
    _, h0f, h0b = lru_pair(uxc, zeros, zeros, ctx_tile)

    pos = _grid_pos_tables(t, d)
    x1 = _ffn(x, m3, lp["g_n1"], wu1, wd1, tile=tile, il_in=False, il_out=True,
              mod_idx=(0, 1, 2), pos=pos)
    u, ux, gl, gts = _inproj(x1, m3, lp["g_n2"], w_in, lp["b_in"], tm=tile, dc=dc, wl=wl,
                             lru_only=False, mod_idx=(3, 4))
    yl, _, _ = lru_pair(ux, h0f, h0b, tile, hf_gl=gl)
    x2 = _mix(u, yl, gts, x1, m3, lp["w_dw"], lp["b_dw"], lp["g_ln"], lp["b_ln"],
              lp["w_conf_out"].astype(_BF16), lp["w_lru_out"].astype(_BF16),
              lp["w_out"].astype(_BF16), tile=tile, gate_idx=5)
    return _ffn(x2, m3, lp["g_n3"], wu2, wd2, tile=tile, il_in=True, il_out=False,
                mod_idx=(6, 7, 8), g_final=g_final)


def _forward(x, c, ctx, c_ctx, params, g_final, *, tile=512, ctx_tile=256):
    depth = params["w_mod"].shape[0]
    assert depth == 1, "only the single-layer (context read-only) block is implemented"
    lp = {k: v[0] for k, v in params.items()}
    return _layer(x, c, ctx, c_ctx, lp, g_final, tile=tile, ctx_tile=ctx_tile)


def kernel(x, c, ctx, c_ctx, w_mod, b_mod, g_n1, w_ffn1_up, w_ffn1_down, g_n2, w_in, b_in, w_dw, b_dw, g_ln, b_ln, w_conf_out, w_lru_conv, b_lru_conv, w_rec_gate, b_rec_gate, w_in_gate, b_in_gate, lru_lambda, w_lru_out, w_out, g_n3, w_ffn2_up, w_ffn2_down, g_final):
    params = dict(w_mod=w_mod, b_mod=b_mod, g_n1=g_n1, w_ffn1_up=w_ffn1_up, w_ffn1_down=w_ffn1_down,
                  g_n2=g_n2, w_in=w_in, b_in=b_in, w_dw=w_dw, b_dw=b_dw, g_ln=g_ln, b_ln=b_ln,
                  w_conf_out=w_conf_out, w_lru_conv=w_lru_conv, b_lru_conv=b_lru_conv,
                  w_rec_gate=w_rec_gate, b_rec_gate=b_rec_gate, w_in_gate=w_in_gate,
                  b_in_gate=b_in_gate, lru_lambda=lru_lambda, w_lru_out=w_lru_out, w_out=w_out,
                  g_n3=g_n3, w_ffn2_up=w_ffn2_up, w_ffn2_down=w_ffn2_down)
    return _forward(x, c, ctx, c_ctx, params, g_final)
```

```python
import functools

import jax
import jax.numpy as jnp
from jax import lax
from jax.experimental import pallas as pl
from jax.experimental.pallas import tpu as pltpu

EPS = 1e-6
LRU_C = 8.0
LOG2_E = 1.4426950408889634
TINY = 1e-30
GRID_W = 64
N_MOD = 9
SUBLANES = 8
LANES = 128
LRU_HALO_PREV = 16
LRU_HALO_NEXT = 8
VMEM_LIMIT_BYTES = 56 * 1024 * 1024

_BF16 = jnp.bfloat16
_F32 = jnp.float32


def _sigmoid(x):
    return 0.5 * jnp.tanh(0.5 * x) + 0.5


def _half_gelu_tanh(x):
    return 0.25 * x * (1.0 + jnp.tanh(0.7978845608028654 * (x + 0.044715 * (x * x * x))))


def _rms_mod(x, g, shift, scale):
    ms = jnp.mean(x * x, axis=-1, keepdims=True)
    return (x * lax.rsqrt(ms + EPS)) * (g * (1.0 + scale)) + shift


def _mod_slice(m_ref, idx, d):
    return m_ref[:, idx * d:(idx + 1) * d]


def _params(n_grid):
    return pltpu.CompilerParams(dimension_semantics=("arbitrary",) * n_grid,
                                vmem_limit_bytes=VMEM_LIMIT_BYTES)


def _mod_kernel(c_ref, w_ref, b_ref, o_ref):
    c = c_ref[...]
    a = (c * _sigmoid(c)).astype(_BF16)
    o_ref[...] = jnp.dot(a, w_ref[...].astype(_BF16), preferred_element_type=_F32) + b_ref[...]


def _modulation(cc, w_mod, b_mod):
    rows, d = cc.shape
    n = w_mod.shape[1]
    tn = d
    return pl.pallas_call(
        _mod_kernel,
        grid=(n // tn,),
        in_specs=[pl.BlockSpec((rows, d), lambda j: (0, 0)),
                  pl.BlockSpec((d, tn), lambda j: (0, j)),
                  pl.BlockSpec((1, tn), lambda j: (0, j))],
        out_specs=pl.BlockSpec((rows, tn), lambda j: (0, j)),
        out_shape=jax.ShapeDtypeStruct((rows, n), _F32),
        compiler_params=_params(1),
        name="modulation",
    )(cc, w_mod, b_mod.reshape(1, n))


def _ffn_kernel(*refs, d, f, fc, il_in, il_out, add_pos, final_norm, mod_idx):
    it = iter(refs)
    x_ref = next(it)
    prow_ref, pcol_ref = (next(it), next(it)) if add_pos else (None, None)
    m_ref = next(it)
    g_ref = next(it)
    wu_ref = next(it)
    wd_ref = next(it)
    gf_ref = next(it) if final_norm else None
    o_ref = next(it)

    tile = x_ref.shape[0]
    lc = tile // SUBLANES
    x = x_ref[...]
    if il_in:
        x = jnp.swapaxes(x.reshape(lc, SUBLANES, d), 0, 1).reshape(tile, d)
    if add_pos:
        col_half = pcol_ref[...]
        x = x + jnp.concatenate(
            [jnp.concatenate([jnp.broadcast_to(prow_ref[q:q + 1, :], col_half.shape), col_half], axis=-1)
             for q in range(tile // GRID_W)], axis=0)
    shift, scale, gate = (_mod_slice(m_ref, i, d) for i in mod_idx)
    hb = _rms_mod(x, g_ref[...], shift, scale).astype(_BF16)

    acc = None
    for c0 in range(0, f, fc):
        gt = jnp.dot(hb, wu_ref[:, c0:c0 + fc].astype(_BF16), preferred_element_type=_F32)
        up = jnp.dot(hb, wu_ref[:, f + c0:f + c0 + fc].astype(_BF16), preferred_element_type=_F32)
        hg = 0.5 * gt
        act = ((hg + hg * jnp.tanh(hg)) * up).astype(_BF16)
        part = jnp.dot(act, wd_ref[c0:c0 + fc, :].astype(_BF16), preferred_element_type=_F32)
        acc = part if acc is None else acc + part
    y = x + (0.5 * gate) * acc
    if final_norm:
        ms = jnp.mean(y * y, axis=-1, keepdims=True)
        y = y * lax.rsqrt(ms + EPS) * gf_ref[...]
    if il_out:
        y = jnp.swapaxes(y.reshape(SUBLANES, lc, d), 0, 1).reshape(tile, d)
    o_ref[...] = y


def _ffn(x, m3, g, wu, wd, *, tile, il_in, il_out, mod_idx, pos=None, g_final=None, fc=256):
    b, t_len, d = x.shape
    f = wd.shape[0]
    n_t = t_len // tile
    row_block = pl.BlockSpec((None, tile, d), lambda t, i: (i, t, 0))
    in_specs = [row_block]
    args = [x]
    if pos is not None:
        assert tile % GRID_W == 0
        pos_row, pos_col = pos
        in_specs += [pl.BlockSpec((tile // GRID_W, d // 2), lambda t, i: (t, 0)),
                     pl.BlockSpec(pos_col.shape, lambda t, i: (0, 0))]
        args += [pos_row, pos_col]
    in_specs += [pl.BlockSpec((None, 1, N_MOD * d), lambda t, i: (i, 0, 0)),
                 pl.BlockSpec((1, d), lambda t, i: (0, 0)),
                 pl.BlockSpec(wu.shape, lambda t, i: (0, 0), pipeline_mode=pl.Buffered(1)),
                 pl.BlockSpec(wd.shape, lambda t, i: (0, 0), pipeline_mode=pl.Buffered(1))]
    args += [m3, g.reshape(1, d), wu, wd]
    if g_final is not None:
        in_specs.append(pl.BlockSpec((1, d), lambda t, i: (0, 0)))
        args.append(g_final.reshape(1, d))
    kern = functools.partial(_ffn_kernel, d=d, f=f, fc=fc, il_in=il_in, il_out=il_out,
                             add_pos=pos is not None, final_norm=g_final is not None,
                             mod_idx=mod_idx)
    return pl.pallas_call(
        kern, grid=(n_t, b), in_specs=in_specs, out_specs=row_block,
        out_shape=jax.ShapeDtypeStruct((b, t_len, d), _F32),
        compiler_params=_params(2), name="ffn",
    )(*args)


def _inproj_kernel(x_ref, m_ref, g_ref, w_ref, b_ref, *out_refs, d, dc, wl, cw, lru_only, mod_idx):
    shift, scale = (_mod_slice(m_ref, i, d) for i in mod_idx)
    hb = _rms_mod(x_ref[...], g_ref[...], shift, scale).astype(_BF16)

    def proj(c0):
        return jnp.dot(hb, w_ref[:, c0:c0 + cw], preferred_element_type=_F32) + b_ref[:, c0:c0 + cw]

    if lru_only:
        (ux_ref,) = out_refs
        for c0 in range(0, wl, cw):
            ux_ref[:, c0:c0 + cw] = proj(c0)
        return
    u_ref, ux_ref, gl_ref, gt_ref = out_refs
    for c0 in range(0, dc, cw):
        u_ref[:, c0:c0 + cw] = (proj(c0) * _sigmoid(proj(dc + c0))).astype(_BF16)
    for c0 in range(0, wl, cw):
        ux_ref[:, c0:c0 + cw] = proj(2 * dc + c0)
        gl_ref[:, c0:c0 + cw] = _half_gelu_tanh(proj(2 * dc + wl + c0)).astype(_BF16)
    for c0 in range(0, 2 * d, cw):
        gt_ref[:, c0:c0 + cw] = _sigmoid(proj(2 * dc + 2 * wl + c0)).astype(_BF16)


def _inproj(x, m3, g, w, bias, *, tm, dc, wl, lru_only, mod_idx, cw=256):
    b, t, d = x.shape
    n = w.shape[1]
    row = lambda width: pl.BlockSpec((None, tm, width), lambda i, j: (i, j, 0))
    if lru_only:
        out_specs = [row(wl)]
        out_shape = [jax.ShapeDtypeStruct((b, t, wl), _F32)]
    else:
        out_specs = [row(dc), row(wl), row(wl), row(2 * d)]
        out_shape = [jax.ShapeDtypeStruct((b, t, dc), _BF16),
                     jax.ShapeDtypeStruct((b, t, wl), _F32),
                     jax.ShapeDtypeStruct((b, t, wl), _BF16),
                     jax.ShapeDtypeStruct((b, t, 2 * d), _BF16)]
    kern = functools.partial(_inproj_kernel, d=d, dc=dc, wl=wl, cw=cw, lru_only=lru_only,
                             mod_idx=mod_idx)
    return pl.pallas_call(
        kern, grid=(b, t // tm),
        in_specs=[row(d),
                  pl.BlockSpec((None, 1, N_MOD * d), lambda i, j: (i, 0, 0)),
                  pl.BlockSpec((1, d), lambda i, j: (0, 0)),
                  pl.BlockSpec((d, n), lambda i, j: (0, 0)),
                  pl.BlockSpec((1, n), lambda i, j: (0, 0))],
        out_specs=out_specs, out_shape=out_shape,
        compiler_params=_params(2), name="inproj",
    )(x, m3, g.reshape(1, d), w, bias.reshape(1, n))


def _lru_block(xr, wg, bias_cols, lam, carry_ref, ln, reverse):
    lc = xr.shape[0] // SUBLANES
    zh = jnp.dot(jnp.concatenate([xr.astype(_BF16), bias_cols], axis=1), wg, preferred_element_type=_F32)
    t_rec = jnp.tanh(zh[:, :LANES])
    t_in = jnp.tanh(zh[:, LANES:])
    y = -lam
    softplus = jnp.maximum(y, 0.0) + jnp.log1p(jnp.exp(-jnp.abs(y)))
    half_c = (-0.5 * LRU_C * LOG2_E) * softplus
    a = jnp.exp2((t_rec + 1.0) * half_c)
    v = 1.0 - a * a
    root = v * lax.rsqrt(jnp.maximum(v, TINY))
    bb = root * ((t_in + 1.0) * xr)

    order = range(lc - 1, -1, -1) if reverse else range(lc)
    h = jnp.zeros((SUBLANES, LANES), _F32)
    p = jnp.ones((SUBLANES, LANES), _F32)
    for j in order:
        aj = a[SUBLANES * j:SUBLANES * (j + 1), :]
        h = aj * h + bb[SUBLANES * j:SUBLANES * (j + 1), :]
        p = aj * p
    sub = lax.broadcasted_iota(jnp.int32, (SUBLANES, LANES), 0)
    hin = carry_ref[:, ln]
    shift = 7 if reverse else 1
    for s in (range(6, -1, -1) if reverse else range(1, SUBLANES)):
        hin = jnp.where(sub == s, pltpu.roll(p * hin + h, shift, 0), hin)
    fin = p * hin + h
    carry_ref[:, ln] = pltpu.roll(fin, shift, 0)
    hs = [None] * lc
    h = hin
    for j in order:
        h = a[SUBLANES * j:SUBLANES * (j + 1), :] * h + bb[SUBLANES * j:SUBLANES * (j + 1), :]
        hs[j] = h
    return jnp.concatenate(hs, axis=0), fin


def _lru_kernel(*refs, tile, groups, wl, n_blocks, reverse, combine, has_xr, emit_xr):
    it = iter(refs)
    if has_xr:
        xr_ref = next(it)
    else:
        ux_ref, pv_ref, nx_ref, cw_ref, cb_ref = (next(it) for _ in range(5))
    wg_ref, lam_ref, h0_ref = (next(it) for _ in range(3))
    hf_ref = next(it) if combine else None
    gl_ref = next(it) if combine else None
    out_ref, hlast_ref = next(it), next(it)
    xr_out_ref = next(it) if emit_xr else None
    if not has_xr:
        e_ref = next(it)
    carry_ref = next(it)

    n_t = pl.num_programs(1)
    i = pl.program_id(1)
    tt = n_t - 1 - i if reverse else i
    hp, hn = LRU_HALO_PREV, LRU_HALO_NEXT

    @pl.when(i == 0)
    def _():
        carry_ref[...] = jnp.broadcast_to(h0_ref[...], carry_ref.shape)

    if not has_xr:
        sub_p = lax.broadcasted_iota(jnp.int32, (hp, wl), 0) % SUBLANES
        sub_n = lax.broadcasted_iota(jnp.int32, (hn, wl), 0)
        for g in range(groups):
            r0 = g * tile
            pv = jnp.where(tt == 0, 0.0, pv_ref[...]) if g == 0 else ux_ref[r0 - hp:r0, :]
            e_ref[g, 0:hp, :] = jnp.where(sub_p == 0, pltpu.roll(pv, hp - 7, 0),
                                          pltpu.roll(ux_ref[r0 + tile - hp:r0 + tile, :], 1, 0))
            e_ref[g, hp:hp + tile, :] = ux_ref[r0:r0 + tile, :]
            nx = (jnp.where(tt == n_t - 1, 0.0, nx_ref[...]) if g == groups - 1
                  else ux_ref[r0 + tile:r0 + tile + hn, :])
            e_ref[g, hp + tile:hp + tile + hn, :] = jnp.where(sub_n == 7, pltpu.roll(nx, 7, 0),
                                                              pltpu.roll(ux_ref[r0:r0 + hn, :], 7, 0))

    lane = lax.broadcasted_iota(jnp.int32, (tile, LANES), 1)
    bias_cols = jnp.where(lane < 2, 1.0, 0.0).astype(_BF16)
    for g in (range(groups - 1, -1, -1) if reverse else range(groups)):
        rows = slice(g * tile, (g + 1) * tile)
        for blk in range(n_blocks):
            ln = slice(blk * LANES, (blk + 1) * LANES)
            if has_xr:
                xr = xr_ref[rows, ln]
            else:
                xr = cb_ref[:, ln]
                for k in range(4):
                    xr = xr + cw_ref[k:k + 1, ln] * e_ref[g, SUBLANES * k:SUBLANES * k + tile, ln]
            if emit_xr:
                xr_out_ref[rows, ln] = xr
            hfull, fin = _lru_block(xr, wg_ref[blk], bias_cols, lam_ref[:, ln], carry_ref, ln, reverse)
            hlast_ref[:, ln] = fin if reverse else pltpu.roll(fin, 1, 0)
            if combine:
                out_ref[rows, ln] = ((hf_ref[rows, ln] + hfull) * gl_ref[rows, ln].astype(_F32)).astype(_BF16)
            else:
                out_ref[rows, ln] = hfull


def _lru(ux, cw, cb, wg, lam, h0, *, tile, reverse, groups=1, hf=None, gl=None, xr=None, emit_xr=False):
    has_xr = xr is not None
    b, t, wl = (xr if has_xr else ux).shape
    step = groups * tile
    n_t = t // step
    n_blocks = wl // LANES
    combine = hf is not None
    hp, hn = LRU_HALO_PREV, LRU_HALO_NEXT

    def tmap(j):
        return n_t - 1 - j if reverse else j

    main = lambda width: pl.BlockSpec((None, step, width), lambda i, j: (i, tmap(j), 0))
    full = lambda a: pl.BlockSpec(a.shape, lambda i, j: (0,) * a.ndim)
    if has_xr:
        in_specs, args = [main(wl)], [xr]
    else:
        prev = pl.BlockSpec((None, hp, wl),
                            lambda i, j: (i, jnp.maximum(tmap(j) * (step // hp) - 1, 0), 0))
        nxt = pl.BlockSpec((None, hn, wl),
                           lambda i, j: (i, jnp.minimum((tmap(j) + 1) * (step // hn), t // hn - 1), 0))
        in_specs, args = [main(wl), prev, nxt, full(cw), full(cb)], [ux, ux, ux, cw, cb]
    in_specs += [full(wg), full(lam), pl.BlockSpec((None, 1, wl), lambda i, j: (i, 0, 0))]
    args += [wg, lam, h0]
    if combine:
        in_specs += [main(wl), main(wl)]
        args += [hf, gl]
    out_specs = [main(wl), pl.BlockSpec((None, SUBLANES, wl), lambda i, j: (i, 0, 0))]
    out_shape = [jax.ShapeDtypeStruct((b, t, wl), _BF16 if combine else _F32),
                 jax.ShapeDtypeStruct((b, SUBLANES, wl), _F32)]
    if emit_xr:
        out_specs.append(main(wl))
        out_shape.append(jax.ShapeDtypeStruct((b, t, wl), _F32))
    scratch = [] if has_xr else [pltpu.VMEM((groups, hp + tile + hn, wl), _F32)]
    scratch.append(pltpu.VMEM((SUBLANES, wl), _F32))
    kern = functools.partial(_lru_kernel, tile=tile, groups=groups, wl=wl, n_blocks=n_blocks,
                             reverse=reverse, combine=combine, has_xr=has_xr, emit_xr=emit_xr)
    return pl.pallas_call(
        kern, grid=(b, n_t), in_specs=in_specs, out_specs=out_specs, out_shape=out_shape,
        scratch_shapes=scratch, compiler_params=_params(2),
        name="lru_bwd" if reverse else "lru_fwd",
    )(*args)


def _mix_kernel(u_ref, up_ref, un_ref, yl_ref, gt_ref, x_ref, m_ref, wdw_ref, bdw_ref,
                gln_ref, bln_ref, wco_ref, wlo_ref, wo_ref, o_ref, e_ref, yc_ref,
                *, tile, d, taps, halo, group, gate_idx):
    n_t = pl.num_programs(1)
    i = pl.program_id(1)
    pad = taps // 2
    lc = tile // SUBLANES
    hb_ = halo // SUBLANES

    sub = lax.broadcasted_iota(jnp.int32, (halo, d), 0) % SUBLANES
    pv = jnp.where(i == 0, 0.0, up_ref[...].astype(_F32))
    e_ref[0:halo, :] = jnp.where(sub == 0, pltpu.roll(pv, halo - 7, 0),
                                 pltpu.roll(u_ref[tile - halo:tile, :].astype(_F32), 1, 0))
    e_ref[halo:halo + tile, :] = u_ref[...].astype(_F32)
    nx = jnp.where(i == n_t - 1, 0.0, un_ref[...].astype(_F32))
    e_ref[halo + tile:2 * halo + tile, :] = jnp.where(
        sub == 7, pltpu.roll(nx, 7, 0), pltpu.roll(u_ref[0:halo, :].astype(_F32), halo - 1, 0))

    rows = group * SUBLANES
    for lg in range(d // LANES):
        ln = slice(lg * LANES, (lg + 1) * LANES)
        bias = jnp.broadcast_to(bdw_ref[:, ln], (SUBLANES, LANES))

        def conv_group(gi, carry, ln=ln, bias=bias):
            r0 = pl.multiple_of(gi * rows, rows)
            acc = [[bias] * group] + [[None] * group for _ in range(3)]
            for m in range(group + taps - 1):
                blk = e_ref[pl.ds(r0 + SUBLANES * (hb_ - pad + m), SUBLANES), ln]
                for j in range(max(0, m - taps + 1), min(group, m + 1)):
                    part, term = acc[(m - j) % 4], wdw_ref[m - j:m - j + 1, ln] * blk
                    part[j] = term if part[j] is None else part[j] + term
            yc_ref[pl.ds(r0, rows), ln] = jnp.concatenate(
                [(acc[0][j] + acc[1][j]) + (acc[2][j] + acc[3][j]) for j in range(group)], axis=0)
            return carry

        lax.fori_loop(0, lc // group, conv_group, 0)

    yc = yc_ref[...]
    mu = jnp.mean(yc, axis=-1, keepdims=True)
    xc = yc - mu
    ln_out = xc * lax.rsqrt(jnp.mean(xc * xc, axis=-1, keepdims=True) + EPS) * gln_ref[...] + bln_ref[...]
    half = 0.5 * ln_out
    act = (half + half * jnp.tanh(half)).astype(_BF16)
    y_conf = jnp.dot(act, wco_ref[...], preferred_element_type=_F32)
    y_lru = jnp.dot(yl_ref[...], wlo_ref[...], preferred_element_type=_F32)
    mixed = (gt_ref[:, :d].astype(_F32) * y_conf + gt_ref[:, d:].astype(_F32) * y_lru).astype(_BF16)
    y = jnp.dot(mixed, wo_ref[...], preferred_element_type=_F32)
    o_ref[...] = x_ref[...] + _mod_slice(m_ref, gate_idx, d) * y


def _mix(u, yl, gates, x1, m3, w_dw, b_dw, g_ln, b_ln, wco, wlo, wo, *, tile, gate_idx,
         halo=128, group=8):
    b, t, d = x1.shape
    wl = yl.shape[-1]
    taps = w_dw.shape[0]
    n_t = t // tile
    assert halo // SUBLANES >= taps // 2 + 1 and tile >= halo and (tile // SUBLANES) % group == 0
    main = lambda width: pl.BlockSpec((None, tile, width), lambda i, j: (i, j, 0))
    prev = pl.BlockSpec((None, halo, d), lambda i, j: (i, jnp.maximum(j * (tile // halo) - 1, 0), 0))
    nxt = pl.BlockSpec((None, halo, d),
                       lambda i, j: (i, jnp.minimum((j + 1) * (tile // halo), t // halo - 1), 0))
    full = lambda a: pl.BlockSpec(a.shape, lambda i, j: (0,) * a.ndim)
    vec = lambda a: a.reshape(1, -1)
    consts = [w_dw, vec(b_dw), vec(g_ln), vec(b_ln), wco, wlo, wo]
    kern = functools.partial(_mix_kernel, tile=tile, d=d, taps=taps, halo=halo, group=group,
                             gate_idx=gate_idx)
    return pl.pallas_call(
        kern, grid=(b, n_t),
        in_specs=[main(d), prev, nxt, main(wl), main(2 * d), main(d),
                  pl.BlockSpec((None, 1, N_MOD * d), lambda i, j: (i, 0, 0))]
                 + [full(a) for a in consts],
        out_specs=main(d),
        out_shape=jax.ShapeDtypeStruct((b, t, d), _F32),
        scratch_shapes=[pltpu.VMEM((tile + 2 * halo, d), _F32), pltpu.VMEM((tile, d), _F32)],
        compiler_params=_params(2), name="mix",
    )(u, u, u, yl, gates, x1, m3, *consts)


def _grid_pos_tables(seq_len, dim):
    q = dim // 4
    omega = 1.0 / (10000.0 ** (jnp.arange(q, dtype=_F32) / q))
    er = jnp.arange(seq_len // GRID_W).astype(_F32)[:, None] * omega
    ec = jnp.arange(GRID_W).astype(_F32)[:, None] * omega
    return (jnp.concatenate([jnp.sin(er), jnp.cos(er)], axis=-1),
            jnp.concatenate([jnp.sin(ec), jnp.cos(ec)], axis=-1))


def _gate_weights(w_rec, b_rec, w_in, b_in, direction):
    n_blocks, bw, _ = w_rec.shape[1:]
    wg = (0.5 * jnp.concatenate([w_rec[direction], w_in[direction]], axis=-1)).astype(_BF16)
    bg = 0.5 * jnp.concatenate([b_rec[direction].reshape(n_blocks, 1, bw),
                                b_in[direction].reshape(n_blocks, 1, bw)], axis=-1)
    hi = bg.astype(_BF16)
    lo = (bg - hi.astype(_F32)).astype(_BF16)
    return jnp.concatenate([wg, hi, lo, jnp.zeros((n_blocks, bw - 2, 2 * bw), _BF16)], axis=1)


def _layer(x, c, ctx, c_ctx, lp, g_final, *, tile, ctx_tile):
    b, t, d = x.shape
    dc = lp["w_dw"].shape[-1]
    wl = lp["w_lru_conv"].shape[-1]
    col_lru = 2 * dc

    cc = jnp.concatenate([c, c_ctx[None, :]], axis=0)
    cc = jnp.pad(cc, ((0, SUBLANES - cc.shape[0] % SUBLANES), (0, 0)))
    m_all = _modulation(cc, lp["w_mod"], lp["b_mod"])
    m3 = m_all[:b, None, :]
    mc3 = jnp.broadcast_to(m_all[b][None, None, :], (b, 1, N_MOD * d))

    wu1, wd1 = lp["w_ffn1_up"], lp["w_ffn1_down"]
    wu2, wd2 = lp["w_ffn2_up"], lp["w_ffn2_down"]
    w_in = lp["w_in"].astype(_BF16)
    lam = lp["lru_lambda"]
    gates = [_gate_weights(lp["w_rec_gate"], lp["b_rec_gate"], lp["w_in_gate"], lp["b_in_gate"], k)
             for k in range(2)]
    cw, cb = lp["w_lru_conv"], lp["b_lru_conv"].reshape(1, wl)

    def lru_pair(ux, h0f, h0b, tl, hf_gl=None, groups=1):
        hf, hf_last, xr = _lru(ux, cw, cb, gates[0], lam[0:1], h0f, tile=tl, groups=groups,
                               reverse=False, emit_xr=True)
        extra = {} if hf_gl is None else dict(hf=hf, gl=hf_gl)
        out, hb_last = _lru(None, None, None, gates[1], lam[1:2], h0b, tile=tl, groups=groups,
                            reverse=True, xr=xr, **extra)
        return out, hf_last[:, 0:1, :], hb_last[:, 0:1, :]

    xc1 = _ffn(ctx, mc3, lp["g_n1"], wu1, wd1, tile=ctx_tile, il_in=False, il_out=True,
               mod_idx=(0, 1, 2))
    (uxc,) = _inproj(xc1, mc3, lp["g_n2"], w_in[:, col_lru:col_lru + wl],
                     lp["b_in"][col_lru:col_lru + wl], tm=ctx_tile, dc=dc, wl=wl,
                     lru_only=True, mod_idx=(3, 4))
    zeros = jnp.zeros((b, 1, wl), _F32)
    _, h0f, h0b = lru_pair(uxc, zeros, zeros, ctx_tile)

    pos = _grid_pos_tables(t, d)
    x1 = _ffn(x, m3, lp["g_n1"], wu1, wd1, tile=tile, il_in=False, il_out=True,
              mod_idx=(0, 1, 2), pos=pos)
    u, ux, gl, gts = _inproj(x1, m3, lp["g_n2"], w_in, lp["b_in"], tm=tile, dc=dc, wl=wl,
                             lru_only=False, mod_idx=(3, 4))
    yl, _, _ = lru_pair(ux, h0f, h0b, tile, hf_gl=gl, groups=2 if (t // tile) % 2 == 0 else 1)
    x2 = _mix(u, yl, gts, x1, m3, lp["w_dw"], lp["b_dw"], lp["g_ln"], lp["b_ln"],
              lp["w_conf_out"].astype(_BF16), lp["w_lru_out"].astype(_BF16),
              lp["w_out"].astype(_BF16), tile=tile, gate_idx=5)
    return _ffn(x2, m3, lp["g_n3"], wu2, wd2, tile=tile, il_in=True, il_out=False,
                mod_idx=(6, 7, 8), g_final=g_final)


def _forward(x, c, ctx, c_ctx, params, g_final, *, tile=512, ctx_tile=256):
    depth = params["w_mod"].shape[0]
    assert depth == 1, "only the single-layer (context read-only) block is implemented"
    lp = {k: v[0] for k, v in params.items()}
    return _layer(x, c, ctx, c_ctx, lp, g_final, tile=tile, ctx_tile=ctx_tile)


def kernel(x, c, ctx, c_ctx, w_mod, b_mod, g_n1, w_ffn1_up, w_ffn1_down, g_n2, w_in, b_in, w_dw, b_dw, g_ln, b_ln, w_conf_out, w_lru_conv, b_lru_conv, w_rec_gate, b_rec_gate, w_in_gate, b_in_gate, lru_lambda, w_lru_out, w_out, g_n3, w_ffn2_up, w_ffn2_down, g_final):
    params = dict(w_mod=w_mod, b_mod=b_mod, g_n1=g_n1, w_ffn1_up=w_ffn1_up, w_ffn1_down=w_ffn1_down,
                  g_n2=g_n2, w_in=w_in, b_in=b_in, w_dw=w_dw, b_dw=b_dw, g_ln=g_ln, b_ln=b_ln,
                  w_conf_out=w_conf_out, w_lru_conv=w_lru_conv, b_lru_conv=b_lru_conv,
                  w_rec_gate=w_rec_gate, b_rec_gate=b_rec_gate, w_in_gate=w_in_gate,
                  b_in_gate=b_in_gate, lru_lambda=lru_lambda, w_lru_out=w_lru_out, w_out=w_out,
                  g_n3=g_n3, w_ffn2_up=w_ffn2_up, w_ffn2_down=w_ffn2_down)
    return _forward(x, c, ctx, c_ctx, params, g_final)
```

```python
import functools

import jax
import jax.numpy as jnp
from jax import lax
from jax.experimental import pallas as pl
from jax.experimental.pallas import tpu as pltpu

EPS = 1e-6
LRU_C = 8.0
LOG2_E = 1.4426950408889634
TINY = 1e-30
GRID_W = 64
N_MOD = 9
SUBLANES = 8
LANES = 128
LRU_HALO_PREV = 16
LRU_HALO_NEXT = 8
VMEM_LIMIT_BYTES = 56 * 1024 * 1024

_BF16 = jnp.bfloat16
_F32 = jnp.float32


def _sigmoid(x):
    return 0.5 * jnp.tanh(0.5 * x) + 0.5


def _half_gelu_tanh(x):
    return 0.25 * x * (1.0 + jnp.tanh(0.7978845608028654 * (x + 0.044715 * (x * x * x))))


def _rms_mod(x, g, shift, scale):
    ms = jnp.mean(x * x, axis=-1, keepdims=True)
    return (x * lax.rsqrt(ms + EPS)) * (g * (1.0 + scale)) + shift


def _mod_slice(m_ref, idx, d):
    return m_ref[:, idx * d:(idx + 1) * d]


def _params(n_grid):
    return pltpu.CompilerParams(dimension_semantics=("arbitrary",) * n_grid,
                                vmem_limit_bytes=VMEM_LIMIT_BYTES)


def _mod_kernel(c_ref, w_ref, b_ref, o_ref):
    c = c_ref[...]
    a = (c * _sigmoid(c)).astype(_BF16)
    o_ref[...] = jnp.dot(a, w_ref[...].astype(_BF16), preferred_element_type=_F32) + b_ref[...]


def _modulation(cc, w_mod, b_mod):
    rows, d = cc.shape
    n = w_mod.shape[1]
    tn = d
    return pl.pallas_call(
        _mod_kernel,
        grid=(n // tn,),
        in_specs=[pl.BlockSpec((rows, d), lambda j: (0, 0)),
                  pl.BlockSpec((d, tn), lambda j: (0, j)),
                  pl.BlockSpec((1, tn), lambda j: (0, j))],
        out_specs=pl.BlockSpec((rows, tn), lambda j: (0, j)),
        out_shape=jax.ShapeDtypeStruct((rows, n), _F32),
        compiler_params=_params(1),
        name="modulation",
    )(cc, w_mod, b_mod.reshape(1, n))


def _ffn_kernel(*refs, d, f, fc, il_in, il_out, add_pos, final_norm, mod_idx):
    it = iter(refs)
    x_ref = next(it)
    prow_ref, pcol_ref = (next(it), next(it)) if add_pos else (None, None)
    m_ref = next(it)
    g_ref = next(it)
    wu_ref = next(it)
    wd_ref = next(it)
    gf_ref = next(it) if final_norm else None
    o_ref = next(it)

    tile = x_ref.shape[0]
    lc = tile // SUBLANES
    x = x_ref[...]
    if il_in:
        x = jnp.swapaxes(x.reshape(lc, SUBLANES, d), 0, 1).reshape(tile, d)
    if add_pos:
        col_half = pcol_ref[...]
        x = x + jnp.concatenate(
            [jnp.concatenate([jnp.broadcast_to(prow_ref[q:q + 1, :], col_half.shape), col_half], axis=-1)
             for q in range(tile // GRID_W)], axis=0)
    shift, scale, gate = (_mod_slice(m_ref, i, d) for i in mod_idx)
    hb = _rms_mod(x, g_ref[...], shift, scale).astype(_BF16)

    acc = None
    for c0 in range(0, f, fc):
        gt = jnp.dot(hb, wu_ref[:, c0:c0 + fc].astype(_BF16), preferred_element_type=_F32)
        up = jnp.dot(hb, wu_ref[:, f + c0:f + c0 + fc].astype(_BF16), preferred_element_type=_F32)
        hg = 0.5 * gt
        act = ((hg + hg * jnp.tanh(hg)) * up).astype(_BF16)
        part = jnp.dot(act, wd_ref[c0:c0 + fc, :].astype(_BF16), preferred_element_type=_F32)
        acc = part if acc is None else acc + part
    y = x + (0.5 * gate) * acc
    if final_norm:
        ms = jnp.mean(y * y, axis=-1, keepdims=True)
        y = y * lax.rsqrt(ms + EPS) * gf_ref[...]
    if il_out:
        y = jnp.swapaxes(y.reshape(SUBLANES, lc, d), 0, 1).reshape(tile, d)
    o_ref[...] = y


def _ffn(x, m3, g, wu, wd, *, tile, il_in, il_out, mod_idx, pos=None, g_final=None, fc=256):
    b, t_len, d = x.shape
    f = wd.shape[0]
    n_t = t_len // tile
    row_block = pl.BlockSpec((None, tile, d), lambda t, i: (i, t, 0))
    in_specs = [row_block]
    args = [x]
    if pos is not None:
        assert tile % GRID_W == 0
        pos_row, pos_col = pos
        in_specs += [pl.BlockSpec((tile // GRID_W, d // 2), lambda t, i: (t, 0)),
                     pl.BlockSpec(pos_col.shape, lambda t, i: (0, 0))]
        args += [pos_row, pos_col]
    in_specs += [pl.BlockSpec((None, 1, N_MOD * d), lambda t, i: (i, 0, 0)),
                 pl.BlockSpec((1, d), lambda t, i: (0, 0)),
                 pl.BlockSpec(wu.shape, lambda t, i: (0, 0), pipeline_mode=pl.Buffered(1)),
                 pl.BlockSpec(wd.shape, lambda t, i: (0, 0), pipeline_mode=pl.Buffered(1))]
    args += [m3, g.reshape(1, d), wu, wd]
    if g_final is not None:
        in_specs.append(pl.BlockSpec((1, d), lambda t, i: (0, 0)))
        args.append(g_final.reshape(1, d))
    kern = functools.partial(_ffn_kernel, d=d, f=f, fc=fc, il_in=il_in, il_out=il_out,
                             add_pos=pos is not None, final_norm=g_final is not None,
                             mod_idx=mod_idx)
    return pl.pallas_call(
        kern, grid=(n_t, b), in_specs=in_specs, out_specs=row_block,
        out_shape=jax.ShapeDtypeStruct((b, t_len, d), _F32),
        compiler_params=_params(2), name="ffn",
    )(*args)


def _inproj_kernel(x_ref, m_ref, g_ref, w_ref, b_ref, *out_refs, d, dc, wl, cw, lru_only, mod_idx):
    shift, scale = (_mod_slice(m_ref, i, d) for i in mod_idx)
    hb = _rms_mod(x_ref[...], g_ref[...], shift, scale).astype(_BF16)

    def proj(c0):
        return jnp.dot(hb, w_ref[:, c0:c0 + cw], preferred_element_type=_F32) + b_ref[:, c0:c0 + cw]

    if lru_only:
        (ux_ref,) = out_refs
        for c0 in range(0, wl, cw):
            ux_ref[:, c0:c0 + cw] = proj(c0)
        return
    u_ref, ux_ref, gl_ref, gt_ref = out_refs
    for c0 in range(0, dc, cw):
        u_ref[:, c0:c0 + cw] = (proj(c0) * _sigmoid(proj(dc + c0))).astype(_BF16)
    for c0 in range(0, wl, cw):
        ux_ref[:, c0:c0 + cw] = proj(2 * dc + c0)
        gl_ref[:, c0:c0 + cw] = _half_gelu_tanh(proj(2 * dc + wl + c0)).astype(_BF16)
    for c0 in range(0, 2 * d, cw):
        gt_ref[:, c0:c0 + cw] = _sigmoid(proj(2 * dc + 2 * wl + c0)).astype(_BF16)


def _inproj(x, m3, g, w, bias, *, tm, dc, wl, lru_only, mod_idx, cw=256):
    b, t, d = x.shape
    n = w.shape[1]
    row = lambda width: pl.BlockSpec((None, tm, width), lambda i, j: (i, j, 0))
    if lru_only:
        out_specs = [row(wl)]
        out_shape = [jax.ShapeDtypeStruct((b, t, wl), _F32)]
    else:
        out_specs = [row(dc), row(wl), row(wl), row(2 * d)]
        out_shape = [jax.ShapeDtypeStruct((b, t, dc), _BF16),
                     jax.ShapeDtypeStruct((b, t, wl), _F32),
                     jax.ShapeDtypeStruct((b, t, wl), _BF16),
                     jax.ShapeDtypeStruct((b, t, 2 * d), _BF16)]
    kern = functools.partial(_inproj_kernel, d=d, dc=dc, wl=wl, cw=cw, lru_only=lru_only,
                             mod_idx=mod_idx)
    return pl.pallas_call(
        kern, grid=(b, t // tm),
        in_specs=[row(d),
                  pl.BlockSpec((None, 1, N_MOD * d), lambda i, j: (i, 0, 0)),
                  pl.BlockSpec((1, d), lambda i, j: (0, 0)),
                  pl.BlockSpec((d, n), lambda i, j: (0, 0), pipeline_mode=pl.Buffered(1)),
                  pl.BlockSpec((1, n), lambda i, j: (0, 0))],
        out_specs=out_specs, out_shape=out_shape,
        compiler_params=_params(2), name="inproj",
    )(x, m3, g.reshape(1, d), w, bias.reshape(1, n))


def _lru_block(xr, wg, bias_cols, lam, carry_ref, ln, reverse):
    lc = xr.shape[0] // SUBLANES
    zh = jnp.dot(jnp.concatenate([xr.astype(_BF16), bias_cols], axis=1), wg, preferred_element_type=_F32)
    t_rec = jnp.tanh(zh[:, :LANES])
    t_in = jnp.tanh(zh[:, LANES:])
    y = -lam
    softplus = jnp.maximum(y, 0.0) + jnp.log1p(jnp.exp(-jnp.abs(y)))
    half_c = (-0.5 * LRU_C * LOG2_E) * softplus
    a = jnp.exp2((t_rec + 1.0) * half_c)
    v = 1.0 - a * a
    root = v * lax.rsqrt(jnp.maximum(v, TINY))
    bb = root * ((t_in + 1.0) * xr)

    order = range(lc - 1, -1, -1) if reverse else range(lc)
    h = jnp.zeros((SUBLANES, LANES), _F32)
    p = jnp.ones((SUBLANES, LANES), _F32)
    for j in order:
        aj = a[SUBLANES * j:SUBLANES * (j + 1), :]
        h = aj * h + bb[SUBLANES * j:SUBLANES * (j + 1), :]
        p = aj * p
    sub = lax.broadcasted_iota(jnp.int32, (SUBLANES, LANES), 0)
    hin = carry_ref[:, ln]
    shift = 7 if reverse else 1
    for s in (range(6, -1, -1) if reverse else range(1, SUBLANES)):
        hin = jnp.where(sub == s, pltpu.roll(p * hin + h, shift, 0), hin)
    fin = p * hin + h
    carry_ref[:, ln] = pltpu.roll(fin, shift, 0)
    hs = [None] * lc
    h = hin
    for j in order:
        h = a[SUBLANES * j:SUBLANES * (j + 1), :] * h + bb[SUBLANES * j:SUBLANES * (j + 1), :]
        hs[j] = h
    return jnp.concatenate(hs, axis=0), fin


def _lru_kernel(*refs, tile, groups, wl, n_blocks, reverse, combine, has_xr, emit_xr):
    it = iter(refs)
    if has_xr:
        xr_ref = next(it)
    else:
        ux_ref, pv_ref, nx_ref, cw_ref, cb_ref = (next(it) for _ in range(5))
    wg_ref, lam_ref, h0_ref = (next(it) for _ in range(3))
    hf_ref = next(it) if combine else None
    gl_ref = next(it) if combine else None
    out_ref, hlast_ref = next(it), next(it)
    xr_out_ref = next(it) if emit_xr else None
    if not has_xr:
        e_ref = next(it)
    carry_ref = next(it)

    n_t = pl.num_programs(1)
    i = pl.program_id(1)
    tt = n_t - 1 - i if reverse else i
    hp, hn = LRU_HALO_PREV, LRU_HALO_NEXT

    @pl.when(i == 0)
    def _():
        carry_ref[...] = jnp.broadcast_to(h0_ref[...], carry_ref.shape)

    if not has_xr:
        sub_p = lax.broadcasted_iota(jnp.int32, (hp, wl), 0) % SUBLANES
        sub_n = lax.broadcasted_iota(jnp.int32, (hn, wl), 0)
        for g in range(groups):
            r0 = g * tile
            pv = jnp.where(tt == 0, 0.0, pv_ref[...]) if g == 0 else ux_ref[r0 - hp:r0, :]
            e_ref[g, 0:hp, :] = jnp.where(sub_p == 0, pltpu.roll(pv, hp - 7, 0),
                                          pltpu.roll(ux_ref[r0 + tile - hp:r0 + tile, :], 1, 0))
            e_ref[g, hp:hp + tile, :] = ux_ref[r0:r0 + tile, :]
            nx = (jnp.where(tt == n_t - 1, 0.0, nx_ref[...]) if g == groups - 1
                  else ux_ref[r0 + tile:r0 + tile + hn, :])
            e_ref[g, hp + tile:hp + tile + hn, :] = jnp.where(sub_n == 7, pltpu.roll(nx, 7, 0),
                                                              pltpu.roll(ux_ref[r0:r0 + hn, :], 7, 0))

    lane = lax.broadcasted_iota(jnp.int32, (tile, LANES), 1)
    bias_cols = jnp.where(lane < 2, 1.0, 0.0).astype(_BF16)
    for g in (range(groups - 1, -1, -1) if reverse else range(groups)):
        rows = slice(g * tile, (g + 1) * tile)
        for blk in range(n_blocks):
            ln = slice(blk * LANES, (blk + 1) * LANES)
            if has_xr:
                xr = xr_ref[rows, ln]
            else:
                xr = cb_ref[:, ln]
                for k in range(4):
                    xr = xr + cw_ref[k:k + 1, ln] * e_ref[g, SUBLANES * k:SUBLANES * k + tile, ln]
            if emit_xr:
                xr_out_ref[rows, ln] = xr
            hfull, fin = _lru_block(xr, wg_ref[blk], bias_cols, lam_ref[:, ln], carry_ref, ln, reverse)
            hlast_ref[:, ln] = fin if reverse else pltpu.roll(fin, 1, 0)
            if combine:
                out_ref[rows, ln] = ((hf_ref[rows, ln] + hfull) * gl_ref[rows, ln].astype(_F32)).astype(_BF16)
            else:
                out_ref[rows, ln] = hfull


def _lru(ux, cw, cb, wg, lam, h0, *, tile, reverse, groups=1, hf=None, gl=None, xr=None, emit_xr=False):
    has_xr = xr is not None
    b, t, wl = (xr if has_xr else ux).shape
    step = groups * tile
    n_t = t // step
    n_blocks = wl // LANES
    combine = hf is not None
    hp, hn = LRU_HALO_PREV, LRU_HALO_NEXT

    def tmap(j):
        return n_t - 1 - j if reverse else j

    main = lambda width: pl.BlockSpec((None, step, width), lambda i, j: (i, tmap(j), 0))
    full = lambda a: pl.BlockSpec(a.shape, lambda i, j: (0,) * a.ndim)
    if has_xr:
        in_specs, args = [main(wl)], [xr]
    else:
        prev = pl.BlockSpec((None, hp, wl),
                            lambda i, j: (i, jnp.maximum(tmap(j) * (step // hp) - 1, 0), 0))
        nxt = pl.BlockSpec((None, hn, wl),
                           lambda i, j: (i, jnp.minimum((tmap(j) + 1) * (step // hn), t // hn - 1), 0))
        in_specs, args = [main(wl), prev, nxt, full(cw), full(cb)], [ux, ux, ux, cw, cb]
    in_specs += [full(wg), full(lam), pl.BlockSpec((None, 1, wl), lambda i, j: (i, 0, 0))]
    args += [wg, lam, h0]
    if combine:
        in_specs += [main(wl), main(wl)]
        args += [hf, gl]
    out_specs = [main(wl), pl.BlockSpec((None, SUBLANES, wl), lambda i, j: (i, 0, 0))]
    out_shape = [jax.ShapeDtypeStruct((b, t, wl), _BF16 if combine else _F32),
                 jax.ShapeDtypeStruct((b, SUBLANES, wl), _F32)]
    if emit_xr:
        out_specs.append(main(wl))
        out_shape.append(jax.ShapeDtypeStruct((b, t, wl), _F32))
    scratch = [] if has_xr else [pltpu.VMEM((groups, hp + tile + hn, wl), _F32)]
    scratch.append(pltpu.VMEM((SUBLANES, wl), _F32))
    kern = functools.partial(_lru_kernel, tile=tile, groups=groups, wl=wl, n_blocks=n_blocks,
                             reverse=reverse, combine=combine, has_xr=has_xr, emit_xr=emit_xr)
    return pl.pallas_call(
        kern, grid=(b, n_t), in_specs=in_specs, out_specs=out_specs, out_shape=out_shape,
        scratch_shapes=scratch, compiler_params=_params(2),
        name="lru_bwd" if reverse else "lru_fwd",
    )(*args)


def _mix_kernel(u_ref, up_ref, un_ref, yl_ref, gt_ref, x_ref, m_ref, wdw_ref, bdw_ref,
                gln_ref, bln_ref, wco_ref, wlo_ref, wo_ref, o_ref, e_ref, yc_ref,
                *, tile, d, taps, halo, group, gate_idx):
    n_t = pl.num_programs(1)
    i = pl.program_id(1)
    pad = taps // 2
    lc = tile // SUBLANES
    hb_ = halo // SUBLANES

    sub = lax.broadcasted_iota(jnp.int32, (halo, d), 0) % SUBLANES
    pv = jnp.where(i == 0, 0.0, up_ref[...].astype(_F32))
    e_ref[0:halo, :] = jnp.where(sub == 0, pltpu.roll(pv, halo - 7, 0),
                                 pltpu.roll(u_ref[tile - halo:tile, :].astype(_F32), 1, 0))
    e_ref[halo:halo + tile, :] = u_ref[...].astype(_F32)
    nx = jnp.where(i == n_t - 1, 0.0, un_ref[...].astype(_F32))
    e_ref[halo + tile:2 * halo + tile, :] = jnp.where(
        sub == 7, pltpu.roll(nx, 7, 0), pltpu.roll(u_ref[0:halo, :].astype(_F32), halo - 1, 0))

    rows = group * SUBLANES
    for lg in range(d // LANES):
        ln = slice(lg * LANES, (lg + 1) * LANES)
        bias = jnp.broadcast_to(bdw_ref[:, ln], (SUBLANES, LANES))

        def conv_group(gi, carry, ln=ln, bias=bias):
            r0 = pl.multiple_of(gi * rows, rows)
            acc = [[bias] * group] + [[None] * group for _ in range(3)]
            for m in range(group + taps - 1):
                blk = e_ref[pl.ds(r0 + SUBLANES * (hb_ - pad + m), SUBLANES), ln]
                for j in range(max(0, m - taps + 1), min(group, m + 1)):
                    part, term = acc[(m - j) % 4], wdw_ref[m - j:m - j + 1, ln] * blk
                    part[j] = term if part[j] is None else part[j] + term
            yc_ref[pl.ds(r0, rows), ln] = jnp.concatenate(
                [(acc[0][j] + acc[1][j]) + (acc[2][j] + acc[3][j]) for j in range(group)], axis=0)
            return carry

        lax.fori_loop(0, lc // group, conv_group, 0)

    yc = yc_ref[...]
    mu = jnp.mean(yc, axis=-1, keepdims=True)
    xc = yc - mu
    ln_out = xc * lax.rsqrt(jnp.mean(xc * xc, axis=-1, keepdims=True) + EPS) * gln_ref[...] + bln_ref[...]
    half = 0.5 * ln_out
    act = (half + half * jnp.tanh(half)).astype(_BF16)
    y_conf = jnp.dot(act, wco_ref[...], preferred_element_type=_F32)
    y_lru = jnp.dot(yl_ref[...], wlo_ref[...], preferred_element_type=_F32)
    mixed = (gt_ref[:, :d].astype(_F32) * y_conf + gt_ref[:, d:].astype(_F32) * y_lru).astype(_BF16)
    y = jnp.dot(mixed, wo_ref[...], preferred_element_type=_F32)
    o_ref[...] = x_ref[...] + _mod_slice(m_ref, gate_idx, d) * y


def _mix(u, yl, gates, x1, m3, w_dw, b_dw, g_ln, b_ln, wco, wlo, wo, *, tile, gate_idx,
         halo=128, group=8):
    b, t, d = x1.shape
    wl = yl.shape[-1]
    taps = w_dw.shape[0]
    n_t = t // tile
    assert halo // SUBLANES >= taps // 2 + 1 and tile >= halo and (tile // SUBLANES) % group == 0
    main = lambda width: pl.BlockSpec((None, tile, width), lambda i, j: (i, j, 0))
    prev = pl.BlockSpec((None, halo, d), lambda i, j: (i, jnp.maximum(j * (tile // halo) - 1, 0), 0))
    nxt = pl.BlockSpec((None, halo, d),
                       lambda i, j: (i, jnp.minimum((j + 1) * (tile // halo), t // halo - 1), 0))
    full = lambda a: pl.BlockSpec(a.shape, lambda i, j: (0,) * a.ndim)
    vec = lambda a: a.reshape(1, -1)
    consts = [w_dw, vec(b_dw), vec(g_ln), vec(b_ln), wco, wlo, wo]
    kern = functools.partial(_mix_kernel, tile=tile, d=d, taps=taps, halo=halo, group=group,
                             gate_idx=gate_idx)
    return pl.pallas_call(
        kern, grid=(b, n_t),
        in_specs=[main(d), prev, nxt, main(wl), main(2 * d), main(d),
                  pl.BlockSpec((None, 1, N_MOD * d), lambda i, j: (i, 0, 0))]
                 + [full(a) for a in consts],
        out_specs=main(d),
        out_shape=jax.ShapeDtypeStruct((b, t, d), _F32),
        scratch_shapes=[pltpu.VMEM((tile + 2 * halo, d), _F32), pltpu.VMEM((tile, d), _F32)],
        compiler_params=_params(2), name="mix",
    )(u, u, u, yl, gates, x1, m3, *consts)


def _grid_pos_tables(seq_len, dim):
    q = dim // 4
    omega = 1.0 / (10000.0 ** (jnp.arange(q, dtype=_F32) / q))
    er = jnp.arange(seq_len // GRID_W).astype(_F32)[:, None] * omega
    ec = jnp.arange(GRID_W).astype(_F32)[:, None] * omega
    return (jnp.concatenate([jnp.sin(er), jnp.cos(er)], axis=-1),
            jnp.concatenate([jnp.sin(ec), jnp.cos(ec)], axis=-1))


def _gate_weights(w_rec, b_rec, w_in, b_in, direction):
    n_blocks, bw, _ = w_rec.shape[1:]
    wg = (0.5 * jnp.concatenate([w_rec[direction], w_in[direction]], axis=-1)).astype(_BF16)
    bg = 0.5 * jnp.concatenate([b_rec[direction].reshape(n_blocks, 1, bw),
                                b_in[direction].reshape(n_blocks, 1, bw)], axis=-1)
    hi = bg.astype(_BF16)
    lo = (bg - hi.astype(_F32)).astype(_BF16)
    return jnp.concatenate([wg, hi, lo, jnp.zeros((n_blocks, bw - 2, 2 * bw), _BF16)], axis=1)


def _layer(x, c, ctx, c_ctx, lp, g_final, *, tile, ctx_tile):
    b, t, d = x.shape
    dc = lp["w_dw"].shape[-1]
    wl = lp["w_lru_conv"].shape[-1]
    col_lru = 2 * dc

    cc = jnp.concatenate([c, c_ctx[None, :]], axis=0)
    cc = jnp.pad(cc, ((0, SUBLANES - cc.shape[0] % SUBLANES), (0, 0)))
    m_all = _modulation(cc, lp["w_mod"], lp["b_mod"])
    m3 = m_all[:b, None, :]
    mc3 = jnp.broadcast_to(m_all[b][None, None, :], (b, 1, N_MOD * d))

    wu1, wd1 = lp["w_ffn1_up"], lp["w_ffn1_down"]
    wu2, wd2 = lp["w_ffn2_up"], lp["w_ffn2_down"]
    w_in = lp["w_in"].astype(_BF16)
    lam = lp["lru_lambda"]
    gates = [_gate_weights(lp["w_rec_gate"], lp["b_rec_gate"], lp["w_in_gate"], lp["b_in_gate"], k)
             for k in range(2)]
    cw, cb = lp["w_lru_conv"], lp["b_lru_conv"].reshape(1, wl)

    def lru_pair(ux, h0f, h0b, tl, hf_gl=None, groups=1):
        hf, hf_last, xr = _lru(ux, cw, cb, gates[0], lam[0:1], h0f, tile=tl, groups=groups,
                               reverse=False, emit_xr=True)
        extra = {} if hf_gl is None else dict(hf=hf, gl=hf_gl)
        out, hb_last = _lru(None, None, None, gates[1], lam[1:2], h0b, tile=tl, groups=groups,
                            reverse=True, xr=xr, **extra)
        return out, hf_last[:, 0:1, :], hb_last[:, 0:1, :]

    xc1 = _ffn(ctx, mc3, lp["g_n1"], wu1, wd1, tile=ctx_tile, il_in=False, il_out=True,
               mod_idx=(0, 1, 2))
    (uxc,) = _inproj(xc1, mc3, lp["g_n2"], w_in[:, col_lru:col_lru + wl],
                     lp["b_in"][col_lru:col_lru + wl], tm=ctx_tile, dc=dc, wl=wl,
                     lru_only=True, mod_idx=(3, 4))
    zeros = jnp.zeros((b, 1, wl), _F32)
    _, h0f, h0b = lru_pair(uxc, zeros, zeros, ctx_tile)

    pos = _grid_pos_tables(t, d)
    x1 = _ffn(x, m3, lp["g_n1"], wu1, wd1, tile=tile, il_in=False, il_out=True,
              mod_idx=(0, 1, 2), pos=pos)
    groups = 2 if (t // tile) % 2 == 0 else 1
    u, ux, gl, gts = _inproj(x1, m3, lp["g_n2"], w_in, lp["b_in"], tm=groups * tile, dc=dc, wl=wl,
                             lru_only=False, mod_idx=(3, 4))
    yl, _, _ = lru_pair(ux, h0f, h0b, tile, hf_gl=gl, groups=groups)
    x2 = _mix(u, yl, gts, x1, m3, lp["w_dw"], lp["b_dw"], lp["g_ln"], lp["b_ln"],
              lp["w_conf_out"].astype(_BF16), lp["w_lru_out"].astype(_BF16),
              lp["w_out"].astype(_BF16), tile=tile, gate_idx=5)
    return _ffn(x2, m3, lp["g_n3"], wu2, wd2, tile=tile, il_in=True, il_out=False,
                mod_idx=(6, 7, 8), g_final=g_final)


def _forward(x, c, ctx, c_ctx, params, g_final, *, tile=512, ctx_tile=256):
    depth = params["w_mod"].shape[0]
    assert depth == 1, "only the single-layer (context read-only) block is implemented"
    lp = {k: v[0] for k, v in params.items()}
    return _layer(x, c, ctx, c_ctx, lp, g_final, tile=tile, ctx_tile=ctx_tile)


def kernel(x, c, ctx, c_ctx, w_mod, b_mod, g_n1, w_ffn1_up, w_ffn1_down, g_n2, w_in, b_in, w_dw, b_dw, g_ln, b_ln, w_conf_out, w_lru_conv, b_lru_conv, w_rec_gate, b_rec_gate, w_in_gate, b_in_gate, lru_lambda, w_lru_out, w_out, g_n3, w_ffn2_up, w_ffn2_down, g_final):
    params = dict(w_mod=w_mod, b_mod=b_mod, g_n1=g_n1, w_ffn1_up=w_ffn1_up, w_ffn1_down=w_ffn1_down,
                  g_n2=g_n2, w_in=w_in, b_in=b_in, w_dw=w_dw, b_dw=b_dw, g_ln=g_ln, b_ln=b_ln,
                  w_conf_out=w_conf_out, w_lru_conv=w_lru_conv, b_lru_conv=b_lru_conv,
                  w_rec_gate=w_rec_gate, b_rec_gate=b_rec_gate, w_in_gate=w_in_gate,
                  b_in_gate=b_in_gate, lru_lambda=lru_lambda, w_lru_out=w_lru_out, w_out=w_out,
                  g_n3=g_n3, w_ffn2_up=w_ffn2_up, w_ffn2_down=w_ffn2_down)
    return _forward(x, c, ctx, c_ctx, params, g_final)
```

```python
import functools

import jax
import jax.numpy as jnp
from jax import lax
from jax.experimental import pallas as pl
from jax.experimental.pallas import tpu as pltpu

EPS = 1e-6
LRU_C = 8.0
LOG2_E = 1.4426950408889634
TINY = 1e-30
GRID_W = 64
N_MOD = 9
SUBLANES = 8
LANES = 128
LRU_HALO_PREV = 16
LRU_HALO_NEXT = 8
VMEM_LIMIT_BYTES = 56 * 1024 * 1024

_BF16 = jnp.bfloat16
_F32 = jnp.float32


def _sigmoid(x):
    return 0.5 * jnp.tanh(0.5 * x) + 0.5


def _half_gelu_tanh(x):
    return 0.25 * x * (1.0 + jnp.tanh(0.7978845608028654 * (x + 0.044715 * (x * x * x))))


def _rms_mod(x, g, shift, scale):
    ms = jnp.mean(x * x, axis=-1, keepdims=True)
    return (x * lax.rsqrt(ms + EPS)) * (g * (1.0 + scale)) + shift


def _mod_slice(m_ref, idx, d):
    return m_ref[:, idx * d:(idx + 1) * d]


def _params(n_grid):
    return pltpu.CompilerParams(dimension_semantics=("arbitrary",) * n_grid,
                                vmem_limit_bytes=VMEM_LIMIT_BYTES)


def _mod_kernel(c_ref, w_ref, b_ref, o_ref):
    c = c_ref[...]
    a = (c * _sigmoid(c)).astype(_BF16)
    o_ref[...] = jnp.dot(a, w_ref[...].astype(_BF16), preferred_element_type=_F32) + b_ref[...]


def _modulation(cc, w_mod, b_mod):
    rows, d = cc.shape
    n = w_mod.shape[1]
    tn = d
    return pl.pallas_call(
        _mod_kernel,
        grid=(n // tn,),
        in_specs=[pl.BlockSpec((rows, d), lambda j: (0, 0)),
                  pl.BlockSpec((d, tn), lambda j: (0, j)),
                  pl.BlockSpec((1, tn), lambda j: (0, j))],
        out_specs=pl.BlockSpec((rows, tn), lambda j: (0, j)),
        out_shape=jax.ShapeDtypeStruct((rows, n), _F32),
        compiler_params=_params(1),
        name="modulation",
    )(cc, w_mod, b_mod.reshape(1, n))


def _ffn_kernel(*refs, d, f, fc, il_in, il_out, add_pos, final_norm, mod_idx):
    it = iter(refs)
    x_ref = next(it)
    prow_ref, pcol_ref = (next(it), next(it)) if add_pos else (None, None)
    m_ref = next(it)
    g_ref = next(it)
    wu_ref = next(it)
    wd_ref = next(it)
    gf_ref = next(it) if final_norm else None
    o_ref = next(it)

    tile = x_ref.shape[0]
    lc = tile // SUBLANES
    x = x_ref[...]
    if il_in:
        x = jnp.swapaxes(x.reshape(lc, SUBLANES, d), 0, 1).reshape(tile, d)
    if add_pos:
        col_half = pcol_ref[...]
        x = x + jnp.concatenate(
            [jnp.concatenate([jnp.broadcast_to(prow_ref[q:q + 1, :], col_half.shape), col_half], axis=-1)
             for q in range(tile // GRID_W)], axis=0)
    shift, scale, gate = (_mod_slice(m_ref, i, d) for i in mod_idx)
    hb = _rms_mod(x, g_ref[...], shift, scale).astype(_BF16)

    acc = None
    for c0 in range(0, f, fc):
        gt = jnp.dot(hb, wu_ref[:, c0:c0 + fc].astype(_BF16), preferred_element_type=_F32)
        up = jnp.dot(hb, wu_ref[:, f + c0:f + c0 + fc].astype(_BF16), preferred_element_type=_F32)
        hg = 0.5 * gt
        act = ((hg + hg * jnp.tanh(hg)) * up).astype(_BF16)
        part = jnp.dot(act, wd_ref[c0:c0 + fc, :].astype(_BF16), preferred_element_type=_F32)
        acc = part if acc is None else acc + part
    y = x + (0.5 * gate) * acc
    if final_norm:
        ms = jnp.mean(y * y, axis=-1, keepdims=True)
        y = y * lax.rsqrt(ms + EPS) * gf_ref[...]
    if il_out:
        y = jnp.swapaxes(y.reshape(SUBLANES, lc, d), 0, 1).reshape(tile, d)
    o_ref[...] = y


def _ffn(x, m3, g, wu, wd, *, tile, il_in, il_out, mod_idx, pos=None, g_final=None, fc=256):
    b, t_len, d = x.shape
    f = wd.shape[0]
    n_t = t_len // tile
    row_block = pl.BlockSpec((None, tile, d), lambda t, i: (i, t, 0))
    in_specs = [row_block]
    args = [x]
    if pos is not None:
        assert tile % GRID_W == 0
        pos_row, pos_col = pos
        in_specs += [pl.BlockSpec((tile // GRID_W, d // 2), lambda t, i: (t, 0)),
                     pl.BlockSpec(pos_col.shape, lambda t, i: (0, 0))]
        args += [pos_row, pos_col]
    in_specs += [pl.BlockSpec((None, 1, N_MOD * d), lambda t, i: (i, 0, 0)),
                 pl.BlockSpec((1, d), lambda t, i: (0, 0)),
                 pl.BlockSpec(wu.shape, lambda t, i: (0, 0), pipeline_mode=pl.Buffered(1)),
                 pl.BlockSpec(wd.shape, lambda t, i: (0, 0), pipeline_mode=pl.Buffered(1))]
    args += [m3, g.reshape(1, d), wu, wd]
    if g_final is not None:
        in_specs.append(pl.BlockSpec((1, d), lambda t, i: (0, 0)))
        args.append(g_final.reshape(1, d))
    kern = functools.partial(_ffn_kernel, d=d, f=f, fc=fc, il_in=il_in, il_out=il_out,
                             add_pos=pos is not None, final_norm=g_final is not None,
                             mod_idx=mod_idx)
    return pl.pallas_call(
        kern, grid=(n_t, b), in_specs=in_specs, out_specs=row_block,
        out_shape=jax.ShapeDtypeStruct((b, t_len, d), _F32),
        compiler_params=_params(2), name="ffn",
    )(*args)


def _inproj_kernel(x_ref, m_ref, g_ref, w_ref, b_ref, *out_refs, d, dc, wl, cw, lru_only, mod_idx):
    shift, scale = (_mod_slice(m_ref, i, d) for i in mod_idx)
    hb = _rms_mod(x_ref[...], g_ref[...], shift, scale).astype(_BF16)

    def proj(c0):
        return jnp.dot(hb, w_ref[:, c0:c0 + cw], preferred_element_type=_F32) + b_ref[:, c0:c0 + cw]

    if lru_only:
        (ux_ref,) = out_refs
        for c0 in range(0, wl, cw):
            ux_ref[:, c0:c0 + cw] = proj(c0)
        return
    u_ref, ux_ref, gl_ref, gt_ref = out_refs
    for c0 in range(0, dc, cw):
        u_ref[:, c0:c0 + cw] = (proj(c0) * _sigmoid(proj(dc + c0))).astype(_BF16)
    for c0 in range(0, wl, cw):
        ux_ref[:, c0:c0 + cw] = proj(2 * dc + c0)
        gl_ref[:, c0:c0 + cw] = _half_gelu_tanh(proj(2 * dc + wl + c0)).astype(_BF16)
    for c0 in range(0, 2 * d, cw):
        gt_ref[:, c0:c0 + cw] = _sigmoid(proj(2 * dc + 2 * wl + c0)).astype(_BF16)


def _inproj(x, m3, g, w, bias, *, tm, dc, wl, lru_only, mod_idx, cw=256):
    b, t, d = x.shape
    n = w.shape[1]
    row = lambda width: pl.BlockSpec((None, tm, width), lambda i, j: (i, j, 0))
    if lru_only:
        out_specs = [row(wl)]
        out_shape = [jax.ShapeDtypeStruct((b, t, wl), _F32)]
    else:
        out_specs = [row(dc), row(wl), row(wl), row(2 * d)]
        out_shape = [jax.ShapeDtypeStruct((b, t, dc), _BF16),
                     jax.ShapeDtypeStruct((b, t, wl), _F32),
                     jax.ShapeDtypeStruct((b, t, wl), _BF16),
                     jax.ShapeDtypeStruct((b, t, 2 * d), _BF16)]
    kern = functools.partial(_inproj_kernel, d=d, dc=dc, wl=wl, cw=cw, lru_only=lru_only,
                             mod_idx=mod_idx)
    return pl.pallas_call(
        kern, grid=(b, t // tm),
        in_specs=[row(d),
                  pl.BlockSpec((None, 1, N_MOD * d), lambda i, j: (i, 0, 0)),
                  pl.BlockSpec((1, d), lambda i, j: (0, 0)),
                  pl.BlockSpec((d, n), lambda i, j: (0, 0), pipeline_mode=pl.Buffered(1)),
                  pl.BlockSpec((1, n), lambda i, j: (0, 0))],
        out_specs=out_specs, out_shape=out_shape,
        compiler_params=_params(2), name="inproj",
    )(x, m3, g.reshape(1, d), w, bias.reshape(1, n))


def _lru_block(xr, wg, bias_cols, lam, carry_ref, ln, reverse):
    lc = xr.shape[0] // SUBLANES
    zh = jnp.dot(jnp.concatenate([xr.astype(_BF16), bias_cols], axis=1), wg, preferred_element_type=_F32)
    t_rec = jnp.tanh(zh[:, :LANES])
    t_in = jnp.tanh(zh[:, LANES:])
    y = -lam
    softplus = jnp.maximum(y, 0.0) + jnp.log1p(jnp.exp(-jnp.abs(y)))
    half_c = (-0.5 * LRU_C * LOG2_E) * softplus
    a = jnp.exp2((t_rec + 1.0) * half_c)
    v = 1.0 - a * a
    root = v * lax.rsqrt(jnp.maximum(v, TINY))
    bb = root * ((t_in + 1.0) * xr)

    order = range(lc - 1, -1, -1) if reverse else range(lc)
    h = jnp.zeros((SUBLANES, LANES), _F32)
    p = jnp.ones((SUBLANES, LANES), _F32)
    for j in order:
        aj = a[SUBLANES * j:SUBLANES * (j + 1), :]
        h = aj * h + bb[SUBLANES * j:SUBLANES * (j + 1), :]
        p = aj * p
    sub = lax.broadcasted_iota(jnp.int32, (SUBLANES, LANES), 0)
    hin = carry_ref[:, ln]
    shift = 7 if reverse else 1
    for s in (range(6, -1, -1) if reverse else range(1, SUBLANES)):
        hin = jnp.where(sub == s, pltpu.roll(p * hin + h, shift, 0), hin)
    fin = p * hin + h
    carry_ref[:, ln] = pltpu.roll(fin, shift, 0)
    hs = [None] * lc
    h = hin
    for j in order:
        h = a[SUBLANES * j:SUBLANES * (j + 1), :] * h + bb[SUBLANES * j:SUBLANES * (j + 1), :]
        hs[j] = h
    return jnp.concatenate(hs, axis=0), fin


def _lru_kernel(*refs, tile, groups, wl, n_blocks, reverse, combine, has_xr, emit_xr):
    it = iter(refs)
    if has_xr:
        xr_ref = next(it)
    else:
        ux_ref, pv_ref, nx_ref, cw_ref, cb_ref = (next(it) for _ in range(5))
    wg_ref, lam_ref, h0_ref = (next(it) for _ in range(3))
    hf_ref = next(it) if combine else None
    gl_ref = next(it) if combine else None
    out_ref, hlast_ref = next(it), next(it)
    xr_out_ref = next(it) if emit_xr else None
    if not has_xr:
        e_ref = next(it)
    carry_ref = next(it)

    n_t = pl.num_programs(1)
    i = pl.program_id(1)
    tt = n_t - 1 - i if reverse else i
    hp, hn = LRU_HALO_PREV, LRU_HALO_NEXT

    @pl.when(i == 0)
    def _():
        carry_ref[...] = jnp.broadcast_to(h0_ref[...], carry_ref.shape)

    if not has_xr:
        sub_p = lax.broadcasted_iota(jnp.int32, (hp, wl), 0) % SUBLANES
        sub_n = lax.broadcasted_iota(jnp.int32, (hn, wl), 0)
        for g in range(groups):
            r0 = g * tile
            pv = jnp.where(tt == 0, 0.0, pv_ref[...]) if g == 0 else ux_ref[r0 - hp:r0, :]
            e_ref[g, 0:hp, :] = jnp.where(sub_p == 0, pltpu.roll(pv, hp - 7, 0),
                                          pltpu.roll(ux_ref[r0 + tile - hp:r0 + tile, :], 1, 0))
            e_ref[g, hp:hp + tile, :] = ux_ref[r0:r0 + tile, :]
            nx = (jnp.where(tt == n_t - 1, 0.0, nx_ref[...]) if g == groups - 1
                  else ux_ref[r0 + tile:r0 + tile + hn, :])
            e_ref[g, hp + tile:hp + tile + hn, :] = jnp.where(sub_n == 7, pltpu.roll(nx, 7, 0),
                                                              pltpu.roll(ux_ref[r0:r0 + hn, :], 7, 0))

    lane = lax.broadcasted_iota(jnp.int32, (tile, LANES), 1)
    bias_cols = jnp.where(lane < 2, 1.0, 0.0).astype(_BF16)
    for g in (range(groups - 1, -1, -1) if reverse else range(groups)):
        rows = slice(g * tile, (g + 1) * tile)
        for blk in range(n_blocks):
            ln = slice(blk * LANES, (blk + 1) * LANES)
            if has_xr:
                xr = xr_ref[rows, ln]
            else:
                xr = cb_ref[:, ln]
                for k in range(4):
                    xr = xr + cw_ref[k:k + 1, ln] * e_ref[g, SUBLANES * k:SUBLANES * k + tile, ln]
            if emit_xr:
                xr_out_ref[rows, ln] = xr
            hfull, fin = _lru_block(xr, wg_ref[blk], bias_cols, lam_ref[:, ln], carry_ref, ln, reverse)
            hlast_ref[:, ln] = fin if reverse else pltpu.roll(fin, 1, 0)
            if combine:
                out_ref[rows, ln] = ((hf_ref[rows, ln] + hfull) * gl_ref[rows, ln].astype(_F32)).astype(_BF16)
            else:
                out_ref[rows, ln] = hfull


def _lru(ux, cw, cb, wg, lam, h0, *, tile, reverse, groups=1, hf=None, gl=None, xr=None, emit_xr=False):
    has_xr = xr is not None
    b, t, wl = (xr if has_xr else ux).shape
    step = groups * tile
    n_t = t // step
    n_blocks = wl // LANES
    combine = hf is not None
    hp, hn = LRU_HALO_PREV, LRU_HALO_NEXT

    def tmap(j):
        return n_t - 1 - j if reverse else j

    main = lambda width: pl.BlockSpec((None, step, width), lambda i, j: (i, tmap(j), 0))
    full = lambda a: pl.BlockSpec(a.shape, lambda i, j: (0,) * a.ndim)
    if has_xr:
        in_specs, args = [main(wl)], [xr]
    else:
        prev = pl.BlockSpec((None, hp, wl),
                            lambda i, j: (i, jnp.maximum(tmap(j) * (step // hp) - 1, 0), 0))
        nxt = pl.BlockSpec((None, hn, wl),
                           lambda i, j: (i, jnp.minimum((tmap(j) + 1) * (step // hn), t // hn - 1), 0))
        in_specs, args = [main(wl), prev, nxt, full(cw), full(cb)], [ux, ux, ux, cw, cb]
    in_specs += [full(wg), full(lam), pl.BlockSpec((None, 1, wl), lambda i, j: (i, 0, 0))]
    args += [wg, lam, h0]
    if combine:
        in_specs += [main(wl), main(wl)]
        args += [hf, gl]
    out_specs = [main(wl), pl.BlockSpec((None, SUBLANES, wl), lambda i, j: (i, 0, 0))]
    out_shape = [jax.ShapeDtypeStruct((b, t, wl), _BF16 if combine else _F32),
                 jax.ShapeDtypeStruct((b, SUBLANES, wl), _F32)]
    if emit_xr:
        out_specs.append(main(wl))
        out_shape.append(jax.ShapeDtypeStruct((b, t, wl), _F32))
    scratch = [] if has_xr else [pltpu.VMEM((groups, hp + tile + hn, wl), _F32)]
    scratch.append(pltpu.VMEM((SUBLANES, wl), _F32))
    kern = functools.partial(_lru_kernel, tile=tile, groups=groups, wl=wl, n_blocks=n_blocks,
                             reverse=reverse, combine=combine, has_xr=has_xr, emit_xr=emit_xr)
    return pl.pallas_call(
        kern, grid=(b, n_t), in_specs=in_specs, out_specs=out_specs, out_shape=out_shape,
        scratch_shapes=scratch, compiler_params=_params(2),
        name="lru_bwd" if reverse else "lru_fwd",
    )(*args)


def _mix_kernel(u_ref, up_ref, un_ref, yl_ref, gt_ref, x_ref, m_ref, wdw_ref, bdw_ref,
                gln_ref, bln_ref, wco_ref, wlo_ref, wo_ref, o_ref, e_ref, yc_ref,
                *, tile, d, taps, halo, group, gate_idx):
    n_t = pl.num_programs(1)
    i = pl.program_id(1)
    pad = taps // 2
    lc = tile // SUBLANES
    hb_ = halo // SUBLANES

    sub = lax.broadcasted_iota(jnp.int32, (halo, d), 0) % SUBLANES
    pv = jnp.where(i == 0, 0.0, up_ref[...].astype(_F32))
    e_ref[0:halo, :] = jnp.where(sub == 0, pltpu.roll(pv, halo - 7, 0),
                                 pltpu.roll(u_ref[tile - halo:tile, :].astype(_F32), 1, 0))
    e_ref[halo:halo + tile, :] = u_ref[...].astype(_F32)
    nx = jnp.where(i == n_t - 1, 0.0, un_ref[...].astype(_F32))
    e_ref[halo + tile:2 * halo + tile, :] = jnp.where(
        sub == 7, pltpu.roll(nx, 7, 0), pltpu.roll(u_ref[0:halo, :].astype(_F32), halo - 1, 0))

    rows = group * SUBLANES
    for lg in range(d // LANES):
        ln = slice(lg * LANES, (lg + 1) * LANES)
        bias = jnp.broadcast_to(bdw_ref[:, ln], (SUBLANES, LANES))

        def conv_group(gi, carry, ln=ln, bias=bias):
            r0 = pl.multiple_of(gi * rows, rows)
            acc = [[bias] * group, [None] * group]
            for m in range(group + taps - 1):
                blk = e_ref[pl.ds(r0 + SUBLANES * (hb_ - pad + m), SUBLANES), ln]
                for j in range(max(0, m - taps + 1), min(group, m + 1)):
                    part, term = acc[(m - j) % 2], wdw_ref[m - j:m - j + 1, ln] * blk
                    part[j] = term if part[j] is None else part[j] + term
            yc_ref[pl.ds(r0, rows), ln] = jnp.concatenate(
                [acc[0][j] + acc[1][j] for j in range(group)], axis=0)
            return carry

        lax.fori_loop(0, lc // group, conv_group, 0)

    yc = yc_ref[...]
    mu = jnp.mean(yc, axis=-1, keepdims=True)
    xc = yc - mu
    ln_out = xc * lax.rsqrt(jnp.mean(xc * xc, axis=-1, keepdims=True) + EPS) * gln_ref[...] + bln_ref[...]
    half = 0.5 * ln_out
    act = (half + half * jnp.tanh(half)).astype(_BF16)
    y_conf = jnp.dot(act, wco_ref[...], preferred_element_type=_F32)
    y_lru = jnp.dot(yl_ref[...], wlo_ref[...], preferred_element_type=_F32)
    mixed = (gt_ref[:, :d].astype(_F32) * y_conf + gt_ref[:, d:].astype(_F32) * y_lru).astype(_BF16)
    y = jnp.dot(mixed, wo_ref[...], preferred_element_type=_F32)
    o_ref[...] = x_ref[...] + _mod_slice(m_ref, gate_idx, d) * y


def _mix(u, yl, gates, x1, m3, w_dw, b_dw, g_ln, b_ln, wco, wlo, wo, *, tile, gate_idx,
         halo=128, group=16):
    b, t, d = x1.shape
    wl = yl.shape[-1]
    taps = w_dw.shape[0]
    n_t = t // tile
    assert halo // SUBLANES >= taps // 2 + 1 and tile >= halo and (tile // SUBLANES) % group == 0
    main = lambda width: pl.BlockSpec((None, tile, width), lambda i, j: (i, j, 0))
    prev = pl.BlockSpec((None, halo, d), lambda i, j: (i, jnp.maximum(j * (tile // halo) - 1, 0), 0))
    nxt = pl.BlockSpec((None, halo, d),
                       lambda i, j: (i, jnp.minimum((j + 1) * (tile // halo), t // halo - 1), 0))
    full = lambda a: pl.BlockSpec(a.shape, lambda i, j: (0,) * a.ndim)
    vec = lambda a: a.reshape(1, -1)
    consts = [w_dw, vec(b_dw), vec(g_ln), vec(b_ln), wco, wlo, wo]
    kern = functools.partial(_mix_kernel, tile=tile, d=d, taps=taps, halo=halo, group=group,
                             gate_idx=gate_idx)
    return pl.pallas_call(
        kern, grid=(b, n_t),
        in_specs=[main(d), prev, nxt, main(wl), main(2 * d), main(d),
                  pl.BlockSpec((None, 1, N_MOD * d), lambda i, j: (i, 0, 0))]
                 + [full(a) for a in consts],
        out_specs=main(d),
        out_shape=jax.ShapeDtypeStruct((b, t, d), _F32),
        scratch_shapes=[pltpu.VMEM((tile + 2 * halo, d), _F32), pltpu.VMEM((tile, d), _F32)],
        compiler_params=_params(2), name="mix",
    )(u, u, u, yl, gates, x1, m3, *consts)


def _grid_pos_tables(seq_len, dim):
    q = dim // 4
    omega = 1.0 / (10000.0 ** (jnp.arange(q, dtype=_F32) / q))
    er = jnp.arange(seq_len // GRID_W).astype(_F32)[:, None] * omega
    ec = jnp.arange(GRID_W).astype(_F32)[:, None] * omega
    return (jnp.concatenate([jnp.sin(er), jnp.cos(er)], axis=-1),
            jnp.concatenate([jnp.sin(ec), jnp.cos(ec)], axis=-1))


def _gate_weights(w_rec, b_rec, w_in, b_in, direction):
    n_blocks, bw, _ = w_rec.shape[1:]
    wg = (0.5 * jnp.concatenate([w_rec[direction], w_in[direction]], axis=-1)).astype(_BF16)
    bg = 0.5 * jnp.concatenate([b_rec[direction].reshape(n_blocks, 1, bw),
                                b_in[direction].reshape(n_blocks, 1, bw)], axis=-1)
    hi = bg.astype(_BF16)
    lo = (bg - hi.astype(_F32)).astype(_BF16)
    return jnp.concatenate([wg, hi, lo, jnp.zeros((n_blocks, bw - 2, 2 * bw), _BF16)], axis=1)


def _layer(x, c, ctx, c_ctx, lp, g_final, *, tile, ctx_tile):
    b, t, d = x.shape
    dc = lp["w_dw"].shape[-1]
    wl = lp["w_lru_conv"].shape[-1]
    col_lru = 2 * dc

    cc = jnp.concatenate([c, c_ctx[None, :]], axis=0)
    cc = jnp.pad(cc, ((0, SUBLANES - cc.shape[0] % SUBLANES), (0, 0)))
    m_all = _modulation(cc, lp["w_mod"], lp["b_mod"])
    m3 = m_all[:b, None, :]
    mc3 = jnp.broadcast_to(m_all[b][None, None, :], (b, 1, N_MOD * d))

    wu1, wd1 = lp["w_ffn1_up"], lp["w_ffn1_down"]
    wu2, wd2 = lp["w_ffn2_up"], lp["w_ffn2_down"]
    w_in = lp["w_in"].astype(_BF16)
    lam = lp["lru_lambda"]
    gates = [_gate_weights(lp["w_rec_gate"], lp["b_rec_gate"], lp["w_in_gate"], lp["b_in_gate"], k)
             for k in range(2)]
    cw, cb = lp["w_lru_conv"], lp["b_lru_conv"].reshape(1, wl)

    def lru_pair(ux, h0f, h0b, tl, hf_gl=None, groups=1):
        hf, hf_last, xr = _lru(ux, cw, cb, gates[0], lam[0:1], h0f, tile=tl, groups=groups,
                               reverse=False, emit_xr=True)
        extra = {} if hf_gl is None else dict(hf=hf, gl=hf_gl)
        out, hb_last = _lru(None, None, None, gates[1], lam[1:2], h0b, tile=tl, groups=groups,
                            reverse=True, xr=xr, **extra)
        return out, hf_last[:, 0:1, :], hb_last[:, 0:1, :]

    xc1 = _ffn(ctx, mc3, lp["g_n1"], wu1, wd1, tile=ctx_tile, il_in=False, il_out=True,
               mod_idx=(0, 1, 2))
    (uxc,) = _inproj(xc1, mc3, lp["g_n2"], w_in[:, col_lru:col_lru + wl],
                     lp["b_in"][col_lru:col_lru + wl], tm=ctx_tile, dc=dc, wl=wl,
                     lru_only=True, mod_idx=(3, 4))
    zeros = jnp.zeros((b, 1, wl), _F32)
    _, h0f, h0b = lru_pair(uxc, zeros, zeros, ctx_tile)

    pos = _grid_pos_tables(t, d)
    x1 = _ffn(x, m3, lp["g_n1"], wu1, wd1, tile=tile, il_in=False, il_out=True,
              mod_idx=(0, 1, 2), pos=pos)
    groups = 2 if (t // tile) % 2 == 0 else 1
    u, ux, gl, gts = _inproj(x1, m3, lp["g_n2"], w_in, lp["b_in"], tm=groups * tile, dc=dc, wl=wl,
                             lru_only=False, mod_idx=(3, 4))
    yl, _, _ = lru_pair(ux, h0f, h0b, tile, hf_gl=gl, groups=groups)
    x2 = _mix(u, yl, gts, x1, m3, lp["w_dw"], lp["b_dw"], lp["g_ln"], lp["b_ln"],
              lp["w_conf_out"].astype(_BF16), lp["w_lru_out"].astype(_BF16),
              lp["w_out"].astype(_BF16), tile=tile, gate_idx=5)
    return _ffn(x2, m3, lp["g_n3"], wu2, wd2, tile=tile, il_in=True, il_out=False,
                mod_idx=(6, 7, 8), g_final=g_final)


def _forward(x, c, ctx, c_ctx, params, g_final, *, tile=512, ctx_tile=256):
    depth = params["w_mod"].shape[0]
    assert depth == 1, "only the single-layer (context read-only) block is implemented"
    lp = {k: v[0] for k, v in params.items()}
    return _layer(x, c, ctx, c_ctx, lp, g_final, tile=tile, ctx_tile=ctx_tile)


def kernel(x, c, ctx, c_ctx, w_mod, b_mod, g_n1, w_ffn1_up, w_ffn1_down, g_n2, w_in, b_in, w_dw, b_dw, g_ln, b_ln, w_conf_out, w_lru_conv, b_lru_conv, w_rec_gate, b_rec_gate, w_in_gate, b_in_gate, lru_lambda, w_lru_out, w_out, g_n3, w_ffn2_up, w_ffn2_down, g_final):
    params = dict(w_mod=w_mod, b_mod=b_mod, g_n1=g_n1, w_ffn1_up=w_ffn1_up, w_ffn1_down=w_ffn1_down,
                  g_n2=g_n2, w_in=w_in, b_in=b_in, w_dw=w_dw, b_dw=b_dw, g_ln=g_ln, b_ln=b_ln,
                  w_conf_out=w_conf_out, w_lru_conv=w_lru_conv, b_lru_conv=b_lru_conv,
                  w_rec_gate=w_rec_gate, b_rec_gate=b_rec_gate, w_in_gate=w_in_gate,
                  b_in_gate=b_in_gate, lru_lambda=lru_lambda, w_lru_out=w_lru_out, w_out=w_out,
                  g_n3=g_n3, w_ffn2_up=w_ffn2_up, w_ffn2_down=w_ffn2_down)
    return _forward(x, c, ctx, c_ctx, params, g_final)
```

```python
import functools

import jax
import jax.numpy as jnp
from jax import lax
from jax.experimental import pallas as pl
from jax.experimental.pallas import tpu as pltpu

EPS = 1e-6
LRU_C = 8.0
LOG2_E = 1.4426950408889634
TINY = 1e-30
GRID_W = 64
N_MOD = 9
SUBLANES = 8
LANES = 128
LRU_HALO_PREV = 16
LRU_HALO_NEXT = 8
VMEM_LIMIT_BYTES = 56 * 1024 * 1024

_BF16 = jnp.bfloat16
_F32 = jnp.float32


def _sigmoid(x):
    return 0.5 * jnp.tanh(0.5 * x) + 0.5


def _half_gelu_tanh(x):
    return 0.25 * x * (1.0 + jnp.tanh(0.7978845608028654 * (x + 0.044715 * (x * x * x))))


def _rms_mod(x, g, shift, scale):
    ms = jnp.mean(x * x, axis=-1, keepdims=True)
    return (x * lax.rsqrt(ms + EPS)) * (g * (1.0 + scale)) + shift


def _mod_slice(m_ref, idx, d):
    return m_ref[:, idx * d:(idx + 1) * d]


def _params(n_grid):
    return pltpu.CompilerParams(dimension_semantics=("arbitrary",) * n_grid,
                                vmem_limit_bytes=VMEM_LIMIT_BYTES)


def _mod_kernel(c_ref, w_ref, b_ref, o_ref):
    c = c_ref[...]
    a = (c * _sigmoid(c)).astype(_BF16)
    o_ref[...] = jnp.dot(a, w_ref[...].astype(_BF16), preferred_element_type=_F32) + b_ref[...]


def _modulation(cc, w_mod, b_mod):
    rows, d = cc.shape
    n = w_mod.shape[1]
    tn = d
    return pl.pallas_call(
        _mod_kernel,
        grid=(n // tn,),
        in_specs=[pl.BlockSpec((rows, d), lambda j: (0, 0)),
                  pl.BlockSpec((d, tn), lambda j: (0, j)),
                  pl.BlockSpec((1, tn), lambda j: (0, j))],
        out_specs=pl.BlockSpec((rows, tn), lambda j: (0, j)),
        out_shape=jax.ShapeDtypeStruct((rows, n), _F32),
        compiler_params=_params(1),
        name="modulation",
    )(cc, w_mod, b_mod.reshape(1, n))


def _ffn_kernel(*refs, d, f, fc, il_in, il_out, add_pos, final_norm, mod_idx):
    it = iter(refs)
    x_ref = next(it)
    prow_ref, pcol_ref = (next(it), next(it)) if add_pos else (None, None)
    m_ref = next(it)
    g_ref = next(it)
    wu_ref = next(it)
    wd_ref = next(it)
    gf_ref = next(it) if final_norm else None
    o_ref = next(it)

    tile = x_ref.shape[0]
    lc = tile // SUBLANES
    x = x_ref[...]
    if il_in:
        x = jnp.swapaxes(x.reshape(lc, SUBLANES, d), 0, 1).reshape(tile, d)
    if add_pos:
        col_half = pcol_ref[...]
        x = x + jnp.concatenate(
            [jnp.concatenate([jnp.broadcast_to(prow_ref[q:q + 1, :], col_half.shape), col_half], axis=-1)
             for q in range(tile // GRID_W)], axis=0)
    shift, scale, gate = (_mod_slice(m_ref, i, d) for i in mod_idx)
    hb = _rms_mod(x, g_ref[...], shift, scale).astype(_BF16)

    acc = None
    for c0 in range(0, f, fc):
        gt = jnp.dot(hb, wu_ref[:, c0:c0 + fc].astype(_BF16), preferred_element_type=_F32)
        up = jnp.dot(hb, wu_ref[:, f + c0:f + c0 + fc].astype(_BF16), preferred_element_type=_F32)
        hg = 0.5 * gt
        act = ((hg + hg * jnp.tanh(hg)) * up).astype(_BF16)
        part = jnp.dot(act, wd_ref[c0:c0 + fc, :].astype(_BF16), preferred_element_type=_F32)
        acc = part if acc is None else acc + part
    y = x + (0.5 * gate) * acc
    if final_norm:
        ms = jnp.mean(y * y, axis=-1, keepdims=True)
        y = y * lax.rsqrt(ms + EPS) * gf_ref[...]
    if il_out:
        y = jnp.swapaxes(y.reshape(SUBLANES, lc, d), 0, 1).reshape(tile, d)
    o_ref[...] = y


def _ffn(x, m3, g, wu, wd, *, tile, il_in, il_out, mod_idx, pos=None, g_final=None, fc=256):
    b, t_len, d = x.shape
    f = wd.shape[0]
    n_t = t_len // tile
    row_block = pl.BlockSpec((None, tile, d), lambda t, i: (i, t, 0))
    in_specs = [row_block]
    args = [x]
    if pos is not None:
        assert tile % GRID_W == 0
        pos_row, pos_col = pos
        in_specs += [pl.BlockSpec((tile // GRID_W, d // 2), lambda t, i: (t, 0)),
                     pl.BlockSpec(pos_col.shape, lambda t, i: (0, 0))]
        args += [pos_row, pos_col]
    in_specs += [pl.BlockSpec((None, 1, N_MOD * d), lambda t, i: (i, 0, 0)),
                 pl.BlockSpec((1, d), lambda t, i: (0, 0)),
                 pl.BlockSpec(wu.shape, lambda t, i: (0, 0), pipeline_mode=pl.Buffered(1)),
                 pl.BlockSpec(wd.shape, lambda t, i: (0, 0), pipeline_mode=pl.Buffered(1))]
    args += [m3, g.reshape(1, d), wu, wd]
    if g_final is not None:
        in_specs.append(pl.BlockSpec((1, d), lambda t, i: (0, 0)))
        args.append(g_final.reshape(1, d))
    kern = functools.partial(_ffn_kernel, d=d, f=f, fc=fc, il_in=il_in, il_out=il_out,
                             add_pos=pos is not None, final_norm=g_final is not None,
                             mod_idx=mod_idx)
    return pl.pallas_call(
        kern, grid=(n_t, b), in_specs=in_specs, out_specs=row_block,
        out_shape=jax.ShapeDtypeStruct((b, t_len, d), _F32),
        compiler_params=_params(2), name="ffn",
    )(*args)


def _inproj_kernel(x_ref, m_ref, g_ref, w_ref, b_ref, *out_refs, d, dc, wl, cw, lru_only, mod_idx):
    shift, scale = (_mod_slice(m_ref, i, d) for i in mod_idx)
    hb = _rms_mod(x_ref[...], g_ref[...], shift, scale).astype(_BF16)

    def proj(c0):
        return jnp.dot(hb, w_ref[:, c0:c0 + cw], preferred_element_type=_F32) + b_ref[:, c0:c0 + cw]

    if lru_only:
        (ux_ref,) = out_refs
        for c0 in range(0, wl, cw):
            ux_ref[:, c0:c0 + cw] = proj(c0)
        return
    u_ref, ux_ref, gl_ref, gt_ref = out_refs
    for c0 in range(0, dc, cw):
        u_ref[:, c0:c0 + cw] = (proj(c0) * _sigmoid(proj(dc + c0))).astype(_BF16)
    for c0 in range(0, wl, cw):
        ux_ref[:, c0:c0 + cw] = proj(2 * dc + c0)
        gl_ref[:, c0:c0 + cw] = _half_gelu_tanh(proj(2 * dc + wl + c0)).astype(_BF16)
    for c0 in range(0, 2 * d, cw):
        gt_ref[:, c0:c0 + cw] = _sigmoid(proj(2 * dc + 2 * wl + c0)).astype(_BF16)


def _inproj(x, m3, g, w, bias, *, tm, dc, wl, lru_only, mod_idx, cw=256):
    b, t, d = x.shape
    n = w.shape[1]
    row = lambda width: pl.BlockSpec((None, tm, width), lambda i, j: (i, j, 0))
    if lru_only:
        out_specs = [row(wl)]
        out_shape = [jax.ShapeDtypeStruct((b, t, wl), _F32)]
    else:
        out_specs = [row(dc), row(wl), row(wl), row(2 * d)]
        out_shape = [jax.ShapeDtypeStruct((b, t, dc), _BF16),
                     jax.ShapeDtypeStruct((b, t, wl), _F32),
                     jax.ShapeDtypeStruct((b, t, wl), _BF16),
                     jax.ShapeDtypeStruct((b, t, 2 * d), _BF16)]
    kern = functools.partial(_inproj_kernel, d=d, dc=dc, wl=wl, cw=cw, lru_only=lru_only,
                             mod_idx=mod_idx)
    return pl.pallas_call(
        kern, grid=(b, t // tm),
        in_specs=[row(d),
                  pl.BlockSpec((None, 1, N_MOD * d), lambda i, j: (i, 0, 0)),
                  pl.BlockSpec((1, d), lambda i, j: (0, 0)),
                  pl.BlockSpec((d, n), lambda i, j: (0, 0), pipeline_mode=pl.Buffered(1)),
                  pl.BlockSpec((1, n), lambda i, j: (0, 0))],
        out_specs=out_specs, out_shape=out_shape,
        compiler_params=_params(2), name="inproj",
    )(x, m3, g.reshape(1, d), w, bias.reshape(1, n))


def _lru_block(xr, wg, bias_cols, lam, carry_ref, ln, reverse):
    lc = xr.shape[0] // SUBLANES
    zh = jnp.dot(jnp.concatenate([xr.astype(_BF16), bias_cols], axis=1), wg, preferred_element_type=_F32)
    t_rec = jnp.tanh(zh[:, :LANES])
    t_in = jnp.tanh(zh[:, LANES:])
    y = -lam
    softplus = jnp.maximum(y, 0.0) + jnp.log1p(jnp.exp(-jnp.abs(y)))
    half_c = (-0.5 * LRU_C * LOG2_E) * softplus
    a = jnp.exp2((t_rec + 1.0) * half_c)
    v = 1.0 - a * a
    root = v * lax.rsqrt(jnp.maximum(v, TINY))
    bb = root * ((t_in + 1.0) * xr)

    order = range(lc - 1, -1, -1) if reverse else range(lc)
    h = jnp.zeros((SUBLANES, LANES), _F32)
    p = jnp.ones((SUBLANES, LANES), _F32)
    for j in order:
        aj = a[SUBLANES * j:SUBLANES * (j + 1), :]
        h = aj * h + bb[SUBLANES * j:SUBLANES * (j + 1), :]
        p = aj * p
    sub = lax.broadcasted_iota(jnp.int32, (SUBLANES, LANES), 0)
    hin = carry_ref[:, ln]
    shift = 7 if reverse else 1
    for s in (range(6, -1, -1) if reverse else range(1, SUBLANES)):
        hin = jnp.where(sub == s, pltpu.roll(p * hin + h, shift, 0), hin)
    fin = p * hin + h
    carry_ref[:, ln] = pltpu.roll(fin, shift, 0)
    hs = [None] * lc
    h = hin
    for j in order:
        h = a[SUBLANES * j:SUBLANES * (j + 1), :] * h + bb[SUBLANES * j:SUBLANES * (j + 1), :]
        hs[j] = h
    return jnp.concatenate(hs, axis=0), fin


def _lru_kernel(*refs, tile, groups, wl, n_blocks, reverse, combine, has_xr, emit_xr):
    it = iter(refs)
    if has_xr:
        xr_ref = next(it)
    else:
        ux_ref, pv_ref, nx_ref, cw_ref, cb_ref = (next(it) for _ in range(5))
    wg_ref, lam_ref, h0_ref = (next(it) for _ in range(3))
    hf_ref = next(it) if combine else None
    gl_ref = next(it) if combine else None
    out_ref, hlast_ref = next(it), next(it)
    xr_out_ref = next(it) if emit_xr else None
    if not has_xr:
        e_ref = next(it)
    carry_ref = next(it)

    n_t = pl.num_programs(1)
    i = pl.program_id(1)
    tt = n_t - 1 - i if reverse else i
    hp, hn = LRU_HALO_PREV, LRU_HALO_NEXT

    @pl.when(i == 0)
    def _():
        carry_ref[...] = jnp.broadcast_to(h0_ref[...], carry_ref.shape)

    if not has_xr:
        sub_p = lax.broadcasted_iota(jnp.int32, (hp, wl), 0) % SUBLANES
        sub_n = lax.broadcasted_iota(jnp.int32, (hn, wl), 0)
        for g in range(groups):
            r0 = g * tile
            pv = jnp.where(tt == 0, 0.0, pv_ref[...]) if g == 0 else ux_ref[r0 - hp:r0, :]
            e_ref[g, 0:hp, :] = jnp.where(sub_p == 0, pltpu.roll(pv, hp - 7, 0),
                                          pltpu.roll(ux_ref[r0 + tile - hp:r0 + tile, :], 1, 0))
            e_ref[g, hp:hp + tile, :] = ux_ref[r0:r0 + tile, :]
            nx = (jnp.where(tt == n_t - 1, 0.0, nx_ref[...]) if g == groups - 1
                  else ux_ref[r0 + tile:r0 + tile + hn, :])
            e_ref[g, hp + tile:hp + tile + hn, :] = jnp.where(sub_n == 7, pltpu.roll(nx, 7, 0),
                                                              pltpu.roll(ux_ref[r0:r0 + hn, :], 7, 0))

    lane = lax.broadcasted_iota(jnp.int32, (tile, LANES), 1)
    bias_cols = jnp.where(lane < 2, 1.0, 0.0).astype(_BF16)
    for g in (range(groups - 1, -1, -1) if reverse else range(groups)):
        rows = slice(g * tile, (g + 1) * tile)
        for blk in range(n_blocks):
            ln = slice(blk * LANES, (blk + 1) * LANES)
            if has_xr:
                xr = xr_ref[rows, ln]
            else:
                xr = cb_ref[:, ln]
                for k in range(4):
                    xr = xr + cw_ref[k:k + 1, ln] * e_ref[g, SUBLANES * k:SUBLANES * k + tile, ln]
            if emit_xr:
                xr_out_ref[rows, ln] = xr
            hfull, fin = _lru_block(xr, wg_ref[blk], bias_cols, lam_ref[:, ln], carry_ref, ln, reverse)
            hlast_ref[:, ln] = fin if reverse else pltpu.roll(fin, 1, 0)
            if combine:
                out_ref[rows, ln] = ((hf_ref[rows, ln] + hfull) * gl_ref[rows, ln].astype(_F32)).astype(_BF16)
            else:
                out_ref[rows, ln] = hfull


def _lru(ux, cw, cb, wg, lam, h0, *, tile, reverse, groups=1, hf=None, gl=None, xr=None, emit_xr=False):
    has_xr = xr is not None
    b, t, wl = (xr if has_xr else ux).shape
    step = groups * tile
    n_t = t // step
    n_blocks = wl // LANES
    combine = hf is not None
    hp, hn = LRU_HALO_PREV, LRU_HALO_NEXT

    def tmap(j):
        return n_t - 1 - j if reverse else j

    main = lambda width: pl.BlockSpec((None, step, width), lambda i, j: (i, tmap(j), 0))
    full = lambda a: pl.BlockSpec(a.shape, lambda i, j: (0,) * a.ndim)
    if has_xr:
        in_specs, args = [main(wl)], [xr]
    else:
        prev = pl.BlockSpec((None, hp, wl),
                            lambda i, j: (i, jnp.maximum(tmap(j) * (step // hp) - 1, 0), 0))
        nxt = pl.BlockSpec((None, hn, wl),
                           lambda i, j: (i, jnp.minimum((tmap(j) + 1) * (step // hn), t // hn - 1), 0))
        in_specs, args = [main(wl), prev, nxt, full(cw), full(cb)], [ux, ux, ux, cw, cb]
    in_specs += [full(wg), full(lam), pl.BlockSpec((None, 1, wl), lambda i, j: (i, 0, 0))]
    args += [wg, lam, h0]
    if combine:
        in_specs += [main(wl), main(wl)]
        args += [hf, gl]
    out_specs = [main(wl), pl.BlockSpec((None, SUBLANES, wl), lambda i, j: (i, 0, 0))]
    out_shape = [jax.ShapeDtypeStruct((b, t, wl), _BF16 if combine else _F32),
                 jax.ShapeDtypeStruct((b, SUBLANES, wl), _F32)]
    if emit_xr:
        out_specs.append(main(wl))
        out_shape.append(jax.ShapeDtypeStruct((b, t, wl), _F32))
    scratch = [] if has_xr else [pltpu.VMEM((groups, hp + tile + hn, wl), _F32)]
    scratch.append(pltpu.VMEM((SUBLANES, wl), _F32))
    kern = functools.partial(_lru_kernel, tile=tile, groups=groups, wl=wl, n_blocks=n_blocks,
                             reverse=reverse, combine=combine, has_xr=has_xr, emit_xr=emit_xr)
    return pl.pallas_call(
        kern, grid=(b, n_t), in_specs=in_specs, out_specs=out_specs, out_shape=out_shape,
        scratch_shapes=scratch, compiler_params=_params(2),
        name="lru_bwd" if reverse else "lru_fwd",
    )(*args)


def _mix_kernel(u_ref, up_ref, un_ref, yl_ref, gt_ref, x_ref, m_ref, wdw_ref, bdw_ref,
                gln_ref, bln_ref, wco_ref, wlo_ref, wo_ref, o_ref, e_ref, yc_ref,
                *, tile, d, taps, halo, group, gate_idx):
    n_t = pl.num_programs(1)
    i = pl.program_id(1)
    pad = taps // 2
    lc = tile // SUBLANES
    hb_ = halo // SUBLANES

    sub = lax.broadcasted_iota(jnp.int32, (halo, d), 0) % SUBLANES
    pv = jnp.where(i == 0, 0.0, up_ref[...].astype(_F32))
    e_ref[0:halo, :] = jnp.where(sub == 0, pltpu.roll(pv, halo - 7, 0),
                                 pltpu.roll(u_ref[tile - halo:tile, :].astype(_F32), 1, 0))
    e_ref[halo:halo + tile, :] = u_ref[...].astype(_F32)
    nx = jnp.where(i == n_t - 1, 0.0, un_ref[...].astype(_F32))
    e_ref[halo + tile:2 * halo + tile, :] = jnp.where(
        sub == 7, pltpu.roll(nx, 7, 0), pltpu.roll(u_ref[0:halo, :].astype(_F32), halo - 1, 0))

    rows = group * SUBLANES
    for lg in range(d // LANES):
        ln = slice(lg * LANES, (lg + 1) * LANES)
        bias = jnp.broadcast_to(bdw_ref[:, ln], (SUBLANES, LANES))

        def conv_group(gi, carry, ln=ln, bias=bias):
            r0 = pl.multiple_of(gi * rows, rows)
            acc = [[bias] * group, [None] * group]
            for m in range(group + taps - 1):
                blk = e_ref[pl.ds(r0 + SUBLANES * (hb_ - pad + m), SUBLANES), ln]
                for j in range(max(0, m - taps + 1), min(group, m + 1)):
                    part, term = acc[(m - j) % 2], wdw_ref[m - j:m - j + 1, ln] * blk
                    part[j] = term if part[j] is None else part[j] + term
            yc_ref[pl.ds(r0, rows), ln] = jnp.concatenate(
                [acc[0][j] + acc[1][j] for j in range(group)], axis=0)
            return carry

        lax.fori_loop(0, lc // group, conv_group, 0)

    yc = yc_ref[...]
    mu = jnp.mean(yc, axis=-1, keepdims=True)
    xc = yc - mu
    ln_out = xc * lax.rsqrt(jnp.mean(xc * xc, axis=-1, keepdims=True) + EPS) * gln_ref[...] + bln_ref[...]
    half = 0.5 * ln_out
    act = (half + half * jnp.tanh(half)).astype(_BF16)
    y_conf = jnp.dot(act, wco_ref[...], preferred_element_type=_F32)
    y_lru = jnp.dot(yl_ref[...], wlo_ref[...], preferred_element_type=_F32)
    mixed = (gt_ref[:, :d].astype(_F32) * y_conf + gt_ref[:, d:].astype(_F32) * y_lru).astype(_BF16)
    y = jnp.dot(mixed, wo_ref[...], preferred_element_type=_F32)
    o_ref[...] = x_ref[...] + _mod_slice(m_ref, gate_idx, d) * y


def _mix(u, yl, gates, x1, m3, w_dw, b_dw, g_ln, b_ln, wco, wlo, wo, *, tile, gate_idx,
         halo=128, group=32):
    b, t, d = x1.shape
    wl = yl.shape[-1]
    taps = w_dw.shape[0]
    n_t = t // tile
    assert halo // SUBLANES >= taps // 2 + 1 and tile >= halo and (tile // SUBLANES) % group == 0
    main = lambda width: pl.BlockSpec((None, tile, width), lambda i, j: (i, j, 0))
    prev = pl.BlockSpec((None, halo, d), lambda i, j: (i, jnp.maximum(j * (tile // halo) - 1, 0), 0))
    nxt = pl.BlockSpec((None, halo, d),
                       lambda i, j: (i, jnp.minimum((j + 1) * (tile // halo), t // halo - 1), 0))
    full = lambda a: pl.BlockSpec(a.shape, lambda i, j: (0,) * a.ndim)
    vec = lambda a: a.reshape(1, -1)
    consts = [w_dw, vec(b_dw), vec(g_ln), vec(b_ln), wco, wlo, wo]
    kern = functools.partial(_mix_kernel, tile=tile, d=d, taps=taps, halo=halo, group=group,
                             gate_idx=gate_idx)
    return pl.pallas_call(
        kern, grid=(b, n_t),
        in_specs=[main(d), prev, nxt, main(wl), main(2 * d), main(d),
                  pl.BlockSpec((None, 1, N_MOD * d), lambda i, j: (i, 0, 0))]
                 + [full(a) for a in consts],
        out_specs=main(d),
        out_shape=jax.ShapeDtypeStruct((b, t, d), _F32),
        scratch_shapes=[pltpu.VMEM((tile + 2 * halo, d), _F32), pltpu.VMEM((tile, d), _F32)],
        compiler_params=_params(2), name="mix",
    )(u, u, u, yl, gates, x1, m3, *consts)


def _grid_pos_tables(seq_len, dim):
    q = dim // 4
    omega = 1.0 / (10000.0 ** (jnp.arange(q, dtype=_F32) / q))
    er = jnp.arange(seq_len // GRID_W).astype(_F32)[:, None] * omega
    ec = jnp.arange(GRID_W).astype(_F32)[:, None] * omega
    return (jnp.concatenate([jnp.sin(er), jnp.cos(er)], axis=-1),
            jnp.concatenate([jnp.sin(ec), jnp.cos(ec)], axis=-1))


def _gate_weights(w_rec, b_rec, w_in, b_in, direction):
    n_blocks, bw, _ = w_rec.shape[1:]
    wg = (0.5 * jnp.concatenate([w_rec[direction], w_in[direction]], axis=-1)).astype(_BF16)
    bg = 0.5 * jnp.concatenate([b_rec[direction].reshape(n_blocks, 1, bw),
                                b_in[direction].reshape(n_blocks, 1, bw)], axis=-1)
    hi = bg.astype(_BF16)
    lo = (bg - hi.astype(_F32)).astype(_BF16)
    return jnp.concatenate([wg, hi, lo, jnp.zeros((n_blocks, bw - 2, 2 * bw), _BF16)], axis=1)


def _layer(x, c, ctx, c_ctx, lp, g_final, *, tile, ctx_tile):
    b, t, d = x.shape
    dc = lp["w_dw"].shape[-1]
    wl = lp["w_lru_conv"].shape[-1]
    col_lru = 2 * dc

    cc = jnp.concatenate([c, c_ctx[None, :]], axis=0)
    cc = jnp.pad(cc, ((0, SUBLANES - cc.shape[0] % SUBLANES), (0, 0)))
    m_all = _modulation(cc, lp["w_mod"], lp["b_mod"])
    m3 = m_all[:b, None, :]
    mc3 = jnp.broadcast_to(m_all[b][None, None, :], (b, 1, N_MOD * d))

    wu1, wd1 = lp["w_ffn1_up"], lp["w_ffn1_down"]
    wu2, wd2 = lp["w_ffn2_up"], lp["w_ffn2_down"]
    w_in = lp["w_in"].astype(_BF16)
    lam = lp["lru_lambda"]
    gates = [_gate_weights(lp["w_rec_gate"], lp["b_rec_gate"], lp["w_in_gate"], lp["b_in_gate"], k)
             for k in range(2)]
    cw, cb = lp["w_lru_conv"], lp["b_lru_conv"].reshape(1, wl)

    def lru_pair(ux, h0f, h0b, tl, hf_gl=None, groups=1):
        hf, hf_last, xr = _lru(ux, cw, cb, gates[0], lam[0:1], h0f, tile=tl, groups=groups,
                               reverse=False, emit_xr=True)
        extra = {} if hf_gl is None else dict(hf=hf, gl=hf_gl)
        out, hb_last = _lru(None, None, None, gates[1], lam[1:2], h0b, tile=tl, groups=groups,
                            reverse=True, xr=xr, **extra)
        return out, hf_last[:, 0:1, :], hb_last[:, 0:1, :]

    xc1 = _ffn(ctx, mc3, lp["g_n1"], wu1, wd1, tile=ctx_tile, il_in=False, il_out=True,
               mod_idx=(0, 1, 2))
    (uxc,) = _inproj(xc1, mc3, lp["g_n2"], w_in[:, col_lru:col_lru + wl],
                     lp["b_in"][col_lru:col_lru + wl], tm=ctx_tile, dc=dc, wl=wl,
                     lru_only=True, mod_idx=(3, 4))
    zeros = jnp.zeros((b, 1, wl), _F32)
    _, h0f, h0b = lru_pair(uxc, zeros, zeros, ctx_tile)

    pos = _grid_pos_tables(t, d)
    x1 = _ffn(x, m3, lp["g_n1"], wu1, wd1, tile=tile, il_in=False, il_out=True,
              mod_idx=(0, 1, 2), pos=pos)
    groups = 2 if (t // tile) % 2 == 0 else 1
    u, ux, gl, gts = _inproj(x1, m3, lp["g_n2"], w_in, lp["b_in"], tm=groups * tile, dc=dc, wl=wl,
                             lru_only=False, mod_idx=(3, 4))
    yl, _, _ = lru_pair(ux, h0f, h0b, tile, hf_gl=gl, groups=groups)
    x2 = _mix(u, yl, gts, x1, m3, lp["w_dw"], lp["b_dw"], lp["g_ln"], lp["b_ln"],
              lp["w_conf_out"].astype(_BF16), lp["w_lru_out"].astype(_BF16),
              lp["w_out"].astype(_BF16), tile=tile, gate_idx=5)
    return _ffn(x2, m3, lp["g_n3"], wu2, wd2, tile=tile, il_in=True, il_out=False,
                mod_idx=(6, 7, 8), g_final=g_final)


def _forward(x, c, ctx, c_ctx, params, g_final, *, tile=512, ctx_tile=256):
    depth = params["w_mod"].shape[0]
    assert depth == 1, "only the single-layer (context read-only) block is implemented"
    lp = {k: v[0] for k, v in params.items()}
    return _layer(x, c, ctx, c_ctx, lp, g_final, tile=tile, ctx_tile=ctx_tile)


def kernel(x, c, ctx, c_ctx, w_mod, b_mod, g_n1, w_ffn1_up, w_ffn1_down, g_n2, w_in, b_in, w_dw, b_dw, g_ln, b_ln, w_conf_out, w_lru_conv, b_lru_conv, w_rec_gate, b_rec_gate, w_in_gate, b_in_gate, lru_lambda, w_lru_out, w_out, g_n3, w_ffn2_up, w_ffn2_down, g_final):
    params = dict(w_mod=w_mod, b_mod=b_mod, g_n1=g_n1, w_ffn1_up=w_ffn1_up, w_ffn1_down=w_ffn1_down,
                  g_n2=g_n2, w_in=w_in, b_in=b_in, w_dw=w_dw, b_dw=b_dw, g_ln=g_ln, b_ln=b_ln,
                  w_conf_out=w_conf_out, w_lru_conv=w_lru_conv, b_lru_conv=b_lru_conv,
                  w_rec_gate=w_rec_gate, b_rec_gate=b_rec_gate, w_in_gate=w_in_gate,
                  b_in_gate=b_in_gate, lru_lambda=lru_lambda, w_lru_out=w_lru_out, w_out=w_out,
                  g_n3=g_n3, w_ffn2_up=w_ffn2_up, w_ffn2_down=w_ffn2_down)
    return _forward(x, c, ctx, c_ctx, params, g_final)
```

```python
import functools

import jax
import jax.numpy as jnp
from jax import lax
from jax.experimental import pallas as pl
from jax.experimental.pallas import tpu as pltpu

EPS = 1e-6
LRU_C = 8.0
LOG2_E = 1.4426950408889634
TINY = 1e-30
GRID_W = 64
N_MOD = 9
SUBLANES = 8
LANES = 128
LRU_HALO_PREV = 16
LRU_HALO_NEXT = 8
VMEM_LIMIT_BYTES = 56 * 1024 * 1024

_BF16 = jnp.bfloat16
_F32 = jnp.float32


def _sigmoid(x):
    return 0.5 * jnp.tanh(0.5 * x) + 0.5


def _half_gelu_tanh(x):
    return 0.25 * x * (1.0 + jnp.tanh(0.7978845608028654 * (x + 0.044715 * (x * x * x))))


def _rms_mod(x, g, shift, scale):
    ms = jnp.mean(x * x, axis=-1, keepdims=True)
    return (x * lax.rsqrt(ms + EPS)) * (g * (1.0 + scale)) + shift


def _mod_slice(m_ref, idx, d):
    return m_ref[:, idx * d:(idx + 1) * d]


def _params(n_grid):
    return pltpu.CompilerParams(dimension_semantics=("arbitrary",) * n_grid,
                                vmem_limit_bytes=VMEM_LIMIT_BYTES)


def _mod_kernel(c_ref, w_ref, b_ref, o_ref):
    c = c_ref[...]
    a = (c * _sigmoid(c)).astype(_BF16)
    o_ref[...] = jnp.dot(a, w_ref[...].astype(_BF16), preferred_element_type=_F32) + b_ref[...]


def _modulation(cc, w_mod, b_mod):
    rows, d = cc.shape
    n = w_mod.shape[1]
    tn = d
    return pl.pallas_call(
        _mod_kernel,
        grid=(n // tn,),
        in_specs=[pl.BlockSpec((rows, d), lambda j: (0, 0)),
                  pl.BlockSpec((d, tn), lambda j: (0, j)),
                  pl.BlockSpec((1, tn), lambda j: (0, j))],
        out_specs=pl.BlockSpec((rows, tn), lambda j: (0, j)),
        out_shape=jax.ShapeDtypeStruct((rows, n), _F32),
        compiler_params=_params(1),
        name="modulation",
    )(cc, w_mod, b_mod.reshape(1, n))


def _ffn_kernel(*refs, d, f, fc, il_in, il_out, add_pos, final_norm, mod_idx):
    it = iter(refs)
    x_ref = next(it)
    prow_ref, pcol_ref = (next(it), next(it)) if add_pos else (None, None)
    m_ref = next(it)
    g_ref = next(it)
    wu_ref = next(it)
    wd_ref = next(it)
    gf_ref = next(it) if final_norm else None
    o_ref = next(it)

    tile = x_ref.shape[0]
    lc = tile // SUBLANES
    x = x_ref[...]
    if il_in:
        x = jnp.swapaxes(x.reshape(lc, SUBLANES, d), 0, 1).reshape(tile, d)
    if add_pos:
        col_half = pcol_ref[...]
        x = x + jnp.concatenate(
            [jnp.concatenate([jnp.broadcast_to(prow_ref[q:q + 1, :], col_half.shape), col_half], axis=-1)
             for q in range(tile // GRID_W)], axis=0)
    shift, scale, gate = (_mod_slice(m_ref, i, d) for i in mod_idx)
    hb = _rms_mod(x, g_ref[...], shift, scale).astype(_BF16)

    acc = None
    for c0 in range(0, f, fc):
        gt = jnp.dot(hb, wu_ref[:, c0:c0 + fc].astype(_BF16), preferred_element_type=_F32)
        up = jnp.dot(hb, wu_ref[:, f + c0:f + c0 + fc].astype(_BF16), preferred_element_type=_F32)
        hg = 0.5 * gt
        act = ((hg + hg * jnp.tanh(hg)) * up).astype(_BF16)
        part = jnp.dot(act, wd_ref[c0:c0 + fc, :].astype(_BF16), preferred_element_type=_F32)
        acc = part if acc is None else acc + part
    y = x + (0.5 * gate) * acc
    if final_norm:
        ms = jnp.mean(y * y, axis=-1, keepdims=True)
        y = y * lax.rsqrt(ms + EPS) * gf_ref[...]
    if il_out:
        y = jnp.swapaxes(y.reshape(SUBLANES, lc, d), 0, 1).reshape(tile, d)
    o_ref[...] = y


def _ffn(x, m3, g, wu, wd, *, tile, il_in, il_out, mod_idx, pos=None, g_final=None, fc=256):
    b, t_len, d = x.shape
    f = wd.shape[0]
    n_t = t_len // tile
    row_block = pl.BlockSpec((None, tile, d), lambda t, i: (i, t, 0))
    in_specs = [row_block]
    args = [x]
    if pos is not None:
        assert tile % GRID_W == 0
        pos_row, pos_col = pos
        in_specs += [pl.BlockSpec((tile // GRID_W, d // 2), lambda t, i: (t, 0)),
                     pl.BlockSpec(pos_col.shape, lambda t, i: (0, 0))]
        args += [pos_row, pos_col]
    in_specs += [pl.BlockSpec((None, 1, N_MOD * d), lambda t, i: (i, 0, 0)),
                 pl.BlockSpec((1, d), lambda t, i: (0, 0)),
                 pl.BlockSpec(wu.shape, lambda t, i: (0, 0), pipeline_mode=pl.Buffered(1)),
                 pl.BlockSpec(wd.shape, lambda t, i: (0, 0), pipeline_mode=pl.Buffered(1))]
    args += [m3, g.reshape(1, d), wu, wd]
    if g_final is not None:
        in_specs.append(pl.BlockSpec((1, d), lambda t, i: (0, 0)))
        args.append(g_final.reshape(1, d))
    kern = functools.partial(_ffn_kernel, d=d, f=f, fc=fc, il_in=il_in, il_out=il_out,
                             add_pos=pos is not None, final_norm=g_final is not None,
                             mod_idx=mod_idx)
    return pl.pallas_call(
        kern, grid=(n_t, b), in_specs=in_specs, out_specs=row_block,
        out_shape=jax.ShapeDtypeStruct((b, t_len, d), _F32),
        compiler_params=_params(2), name="ffn",
    )(*args)


def _inproj_kernel(x_ref, m_ref, g_ref, w_ref, b_ref, *out_refs, d, dc, wl, cw, lru_only, mod_idx):
    shift, scale = (_mod_slice(m_ref, i, d) for i in mod_idx)
    hb = _rms_mod(x_ref[...], g_ref[...], shift, scale).astype(_BF16)

    def proj(c0):
        return jnp.dot(hb, w_ref[:, c0:c0 + cw], preferred_element_type=_F32) + b_ref[:, c0:c0 + cw]

    if lru_only:
        (ux_ref,) = out_refs
        for c0 in range(0, wl, cw):
            ux_ref[:, c0:c0 + cw] = proj(c0)
        return
    u_ref, ux_ref, gl_ref, gt_ref = out_refs
    for c0 in range(0, dc, cw):
        u_ref[:, c0:c0 + cw] = (proj(c0) * _sigmoid(proj(dc + c0))).astype(_BF16)
    for c0 in range(0, wl, cw):
        ux_ref[:, c0:c0 + cw] = proj(2 * dc + c0)
        gl_ref[:, c0:c0 + cw] = _half_gelu_tanh(proj(2 * dc + wl + c0)).astype(_BF16)
    for c0 in range(0, 2 * d, cw):
        gt_ref[:, c0:c0 + cw] = _sigmoid(proj(2 * dc + 2 * wl + c0)).astype(_BF16)


def _inproj(x, m3, g, w, bias, *, tm, dc, wl, lru_only, mod_idx, cw=256):
    b, t, d = x.shape
    n = w.shape[1]
    row = lambda width: pl.BlockSpec((None, tm, width), lambda i, j: (i, j, 0))
    if lru_only:
        out_specs = [row(wl)]
        out_shape = [jax.ShapeDtypeStruct((b, t, wl), _F32)]
    else:
        out_specs = [row(dc), row(wl), row(wl), row(2 * d)]
        out_shape = [jax.ShapeDtypeStruct((b, t, dc), _BF16),
                     jax.ShapeDtypeStruct((b, t, wl), _F32),
                     jax.ShapeDtypeStruct((b, t, wl), _BF16),
                     jax.ShapeDtypeStruct((b, t, 2 * d), _BF16)]
    kern = functools.partial(_inproj_kernel, d=d, dc=dc, wl=wl, cw=cw, lru_only=lru_only,
                             mod_idx=mod_idx)
    return pl.pallas_call(
        kern, grid=(b, t // tm),
        in_specs=[row(d),
                  pl.BlockSpec((None, 1, N_MOD * d), lambda i, j: (i, 0, 0)),
                  pl.BlockSpec((1, d), lambda i, j: (0, 0)),
                  pl.BlockSpec((d, n), lambda i, j: (0, 0), pipeline_mode=pl.Buffered(1)),
                  pl.BlockSpec((1, n), lambda i, j: (0, 0))],
        out_specs=out_specs, out_shape=out_shape,
        compiler_params=_params(2), name="inproj",
    )(x, m3, g.reshape(1, d), w, bias.reshape(1, n))


def _lru_block(xr, wg, bias_cols, lam, carry_ref, ln, reverse):
    lc = xr.shape[0] // SUBLANES
    zh = jnp.dot(jnp.concatenate([xr.astype(_BF16), bias_cols], axis=1), wg, preferred_element_type=_F32)
    t_rec = jnp.tanh(zh[:, :LANES])
    t_in = jnp.tanh(zh[:, LANES:])
    y = -lam
    softplus = jnp.maximum(y, 0.0) + jnp.log1p(jnp.exp(-jnp.abs(y)))
    half_c = (-0.5 * LRU_C * LOG2_E) * softplus
    a = jnp.exp2((t_rec + 1.0) * half_c)
    v = 1.0 - a * a
    root = v * lax.rsqrt(jnp.maximum(v, TINY))
    bb = root * ((t_in + 1.0) * xr)

    order = range(lc - 1, -1, -1) if reverse else range(lc)
    h = jnp.zeros((SUBLANES, LANES), _F32)
    p = jnp.ones((SUBLANES, LANES), _F32)
    for j in order:
        aj = a[SUBLANES * j:SUBLANES * (j + 1), :]
        h = aj * h + bb[SUBLANES * j:SUBLANES * (j + 1), :]
        p = aj * p
    sub = lax.broadcasted_iota(jnp.int32, (SUBLANES, LANES), 0)
    hin = carry_ref[:, ln]
    shift = 7 if reverse else 1
    for s in (range(6, -1, -1) if reverse else range(1, SUBLANES)):
        hin = jnp.where(sub == s, pltpu.roll(p * hin + h, shift, 0), hin)
    fin = p * hin + h
    carry_ref[:, ln] = pltpu.roll(fin, shift, 0)
    hs = [None] * lc
    h = hin
    for j in order:
        h = a[SUBLANES * j:SUBLANES * (j + 1), :] * h + bb[SUBLANES * j:SUBLANES * (j + 1), :]
        hs[j] = h
    return jnp.concatenate(hs, axis=0), fin


def _lru_kernel(*refs, tile, groups, wl, n_blocks, reverse, combine, has_xr, emit_xr):
    it = iter(refs)
    if has_xr:
        xr_ref = next(it)
    else:
        ux_ref, pv_ref, nx_ref, cw_ref, cb_ref = (next(it) for _ in range(5))
    wg_ref, lam_ref, h0_ref = (next(it) for _ in range(3))
    hf_ref = next(it) if combine else None
    gl_ref = next(it) if combine else None
    out_ref, hlast_ref = next(it), next(it)
    xr_out_ref = next(it) if emit_xr else None
    if not has_xr:
        e_ref = next(it)
    carry_ref = next(it)

    n_t = pl.num_programs(1)
    i = pl.program_id(1)
    tt = n_t - 1 - i if reverse else i
    hp, hn = LRU_HALO_PREV, LRU_HALO_NEXT

    @pl.when(i == 0)
    def _():
        carry_ref[...] = jnp.broadcast_to(h0_ref[...], carry_ref.shape)

    if not has_xr:
        sub_p = lax.broadcasted_iota(jnp.int32, (hp, wl), 0) % SUBLANES
        sub_n = lax.broadcasted_iota(jnp.int32, (hn, wl), 0)
        for g in range(groups):
            r0 = g * tile
            pv = jnp.where(tt == 0, 0.0, pv_ref[...]) if g == 0 else ux_ref[r0 - hp:r0, :]
            e_ref[g, 0:hp, :] = jnp.where(sub_p == 0, pltpu.roll(pv, hp - 7, 0),
                                          pltpu.roll(ux_ref[r0 + tile - hp:r0 + tile, :], 1, 0))
            e_ref[g, hp:hp + tile, :] = ux_ref[r0:r0 + tile, :]
            nx = (jnp.where(tt == n_t - 1, 0.0, nx_ref[...]) if g == groups - 1
                  else ux_ref[r0 + tile:r0 + tile + hn, :])
            e_ref[g, hp + tile:hp + tile + hn, :] = jnp.where(sub_n == 7, pltpu.roll(nx, 7, 0),
                                                              pltpu.roll(ux_ref[r0:r0 + hn, :], 7, 0))

    lane = lax.broadcasted_iota(jnp.int32, (tile, LANES), 1)
    bias_cols = jnp.where(lane < 2, 1.0, 0.0).astype(_BF16)
    for g in (range(groups - 1, -1, -1) if reverse else range(groups)):
        rows = slice(g * tile, (g + 1) * tile)
        for blk in range(n_blocks):
            ln = slice(blk * LANES, (blk + 1) * LANES)
            if has_xr:
                xr = xr_ref[rows, ln]
            else:
                xr = cb_ref[:, ln]
                for k in range(4):
                    xr = xr + cw_ref[k:k + 1, ln] * e_ref[g, SUBLANES * k:SUBLANES * k + tile, ln]
            if emit_xr:
                xr_out_ref[rows, ln] = xr
            hfull, fin = _lru_block(xr, wg_ref[blk], bias_cols, lam_ref[:, ln], carry_ref, ln, reverse)
            hlast_ref[:, ln] = fin if reverse else pltpu.roll(fin, 1, 0)
            if combine:
                out_ref[rows, ln] = ((hf_ref[rows, ln] + hfull) * gl_ref[rows, ln].astype(_F32)).astype(_BF16)
            else:
                out_ref[rows, ln] = hfull


def _lru(ux, cw, cb, wg, lam, h0, *, tile, reverse, groups=1, hf=None, gl=None, xr=None, emit_xr=False):
    has_xr = xr is not None
    b, t, wl = (xr if has_xr else ux).shape
    step = groups * tile
    n_t = t // step
    n_blocks = wl // LANES
    combine = hf is not None
    hp, hn = LRU_HALO_PREV, LRU_HALO_NEXT

    def tmap(j):
        return n_t - 1 - j if reverse else j

    main = lambda width: pl.BlockSpec((None, step, width), lambda i, j: (i, tmap(j), 0))
    full = lambda a: pl.BlockSpec(a.shape, lambda i, j: (0,) * a.ndim)
    if has_xr:
        in_specs, args = [main(wl)], [xr]
    else:
        prev = pl.BlockSpec((None, hp, wl),
                            lambda i, j: (i, jnp.maximum(tmap(j) * (step // hp) - 1, 0), 0))
        nxt = pl.BlockSpec((None, hn, wl),
                           lambda i, j: (i, jnp.minimum((tmap(j) + 1) * (step // hn), t // hn - 1), 0))
        in_specs, args = [main(wl), prev, nxt, full(cw), full(cb)], [ux, ux, ux, cw, cb]
    in_specs += [full(wg), full(lam), pl.BlockSpec((None, 1, wl), lambda i, j: (i, 0, 0))]
    args += [wg, lam, h0]
    if combine:
        in_specs += [main(wl), main(wl)]
        args += [hf, gl]
    out_specs = [main(wl), pl.BlockSpec((None, SUBLANES, wl), lambda i, j: (i, 0, 0))]
    out_shape = [jax.ShapeDtypeStruct((b, t, wl), _BF16 if combine else _F32),
                 jax.ShapeDtypeStruct((b, SUBLANES, wl), _F32)]
    if emit_xr:
        out_specs.append(main(wl))
        out_shape.append(jax.ShapeDtypeStruct((b, t, wl), _F32))
    scratch = [] if has_xr else [pltpu.VMEM((groups, hp + tile + hn, wl), _F32)]
    scratch.append(pltpu.VMEM((SUBLANES, wl), _F32))
    kern = functools.partial(_lru_kernel, tile=tile, groups=groups, wl=wl, n_blocks=n_blocks,
                             reverse=reverse, combine=combine, has_xr=has_xr, emit_xr=emit_xr)
    return pl.pallas_call(
        kern, grid=(b, n_t), in_specs=in_specs, out_specs=out_specs, out_shape=out_shape,
        scratch_shapes=scratch, compiler_params=_params(2),
        name="lru_bwd" if reverse else "lru_fwd",
    )(*args)


def _mix_kernel(u_ref, up_ref, un_ref, wdw_ref, bdw_ref, gln_ref, bln_ref, act_ref, e_ref, yc_ref,
                *, tile, d, taps, halo, group):
    n_t = pl.num_programs(1)
    i = pl.program_id(1)
    pad = taps // 2
    lc = tile // SUBLANES
    hb_ = halo // SUBLANES

    sub = lax.broadcasted_iota(jnp.int32, (halo, d), 0) % SUBLANES
    pv = jnp.where(i == 0, 0.0, up_ref[...].astype(_F32))
    e_ref[0:halo, :] = jnp.where(sub == 0, pltpu.roll(pv, halo - 7, 0),
                                 pltpu.roll(u_ref[tile - halo:tile, :].astype(_F32), 1, 0))
    e_ref[halo:halo + tile, :] = u_ref[...].astype(_F32)
    nx = jnp.where(i == n_t - 1, 0.0, un_ref[...].astype(_F32))
    e_ref[halo + tile:2 * halo + tile, :] = jnp.where(
        sub == 7, pltpu.roll(nx, 7, 0), pltpu.roll(u_ref[0:halo, :].astype(_F32), halo - 1, 0))

    rows = group * SUBLANES
    for lg in range(d // LANES):
        ln = slice(lg * LANES, (lg + 1) * LANES)
        bias = jnp.broadcast_to(bdw_ref[:, ln], (SUBLANES, LANES))

        def conv_group(gi, carry, ln=ln, bias=bias):
            r0 = pl.multiple_of(gi * rows, rows)
            acc = [[bias] * group, [None] * group]
            for m in range(group + taps - 1):
                blk = e_ref[pl.ds(r0 + SUBLANES * (hb_ - pad + m), SUBLANES), ln]
                for j in range(max(0, m - taps + 1), min(group, m + 1)):
                    part, term = acc[(m - j) % 2], wdw_ref[m - j:m - j + 1, ln] * blk
                    part[j] = term if part[j] is None else part[j] + term
            yc_ref[pl.ds(r0, rows), ln] = jnp.concatenate(
                [acc[0][j] + acc[1][j] for j in range(group)], axis=0)
            return carry

        lax.fori_loop(0, lc // group, conv_group, 0)

    yc = yc_ref[...]
    mu = jnp.mean(yc, axis=-1, keepdims=True)
    xc = yc - mu
    ln_out = xc * lax.rsqrt(jnp.mean(xc * xc, axis=-1, keepdims=True) + EPS) * gln_ref[...] + bln_ref[...]
    half = 0.5 * ln_out
    act_ref[...] = (half + half * jnp.tanh(half)).astype(_BF16)


def _merge_kernel(act_ref, yl_ref, gt_ref, x_ref, m_ref, wco_ref, wlo_ref, wo_ref, o_ref, *, d, gate_idx):
    y_conf = jnp.dot(act_ref[...], wco_ref[...], preferred_element_type=_F32)
    y_lru = jnp.dot(yl_ref[...], wlo_ref[...], preferred_element_type=_F32)
    mixed = (gt_ref[:, :d].astype(_F32) * y_conf + gt_ref[:, d:].astype(_F32) * y_lru).astype(_BF16)
    y = jnp.dot(mixed, wo_ref[...], preferred_element_type=_F32)
    o_ref[...] = x_ref[...] + _mod_slice(m_ref, gate_idx, d) * y


def _mix(u, yl, gates, x1, m3, w_dw, b_dw, g_ln, b_ln, wco, wlo, wo, *, tile, gate_idx,
         halo=128, group=32):
    b, t, d = x1.shape
    wl = yl.shape[-1]
    taps = w_dw.shape[0]
    n_t = t // tile
    assert halo // SUBLANES >= taps // 2 + 1 and tile >= halo and (tile // SUBLANES) % group == 0
    main = lambda width: pl.BlockSpec((None, tile, width), lambda i, j: (i, j, 0))
    prev = pl.BlockSpec((None, halo, d), lambda i, j: (i, jnp.maximum(j * (tile // halo) - 1, 0), 0))
    nxt = pl.BlockSpec((None, halo, d),
                       lambda i, j: (i, jnp.minimum((j + 1) * (tile // halo), t // halo - 1), 0))
    full = lambda a: pl.BlockSpec(a.shape, lambda i, j: (0,) * a.ndim)
    vec = lambda a: a.reshape(1, -1)
    consts = [w_dw, vec(b_dw), vec(g_ln), vec(b_ln)]
    kern = functools.partial(_mix_kernel, tile=tile, d=d, taps=taps, halo=halo, group=group)
    act = pl.pallas_call(
        kern, grid=(b, n_t),
        in_specs=[main(d), prev, nxt] + [full(a) for a in consts],
        out_specs=main(d),
        out_shape=jax.ShapeDtypeStruct((b, t, d), _BF16),
        scratch_shapes=[pltpu.VMEM((tile + 2 * halo, d), _F32), pltpu.VMEM((tile, d), _F32)],
        compiler_params=_params(2), name="conv",
    )(u, u, u, *consts)
    rows = 2 * tile if n_t % 2 == 0 else tile
    blk = lambda width: pl.BlockSpec((None, rows, width), lambda i, j: (i, j, 0))
    weights = [wco, wlo, wo]
    return pl.pallas_call(
        functools.partial(_merge_kernel, d=d, gate_idx=gate_idx), grid=(b, t // rows),
        in_specs=[blk(d), blk(wl), blk(2 * d), blk(d),
                  pl.BlockSpec((None, 1, N_MOD * d), lambda i, j: (i, 0, 0))]
                 + [full(a) for a in weights],
        out_specs=blk(d),
        out_shape=jax.ShapeDtypeStruct((b, t, d), _F32),
        compiler_params=_params(2), name="merge",
    )(act, yl, gates, x1, m3, *weights)


def _grid_pos_tables(seq_len, dim):
    q = dim // 4
    omega = 1.0 / (10000.0 ** (jnp.arange(q, dtype=_F32) / q))
    er = jnp.arange(seq_len // GRID_W).astype(_F32)[:, None] * omega
    ec = jnp.arange(GRID_W).astype(_F32)[:, None] * omega
    return (jnp.concatenate([jnp.sin(er), jnp.cos(er)], axis=-1),
            jnp.concatenate([jnp.sin(ec), jnp.cos(ec)], axis=-1))


def _gate_weights(w_rec, b_rec, w_in, b_in, direction):
    n_blocks, bw, _ = w_rec.shape[1:]
    wg = (0.5 * jnp.concatenate([w_rec[direction], w_in[direction]], axis=-1)).astype(_BF16)
    bg = 0.5 * jnp.concatenate([b_rec[direction].reshape(n_blocks, 1, bw),
                                b_in[direction].reshape(n_blocks, 1, bw)], axis=-1)
    hi = bg.astype(_BF16)
    lo = (bg - hi.astype(_F32)).astype(_BF16)
    return jnp.concatenate([wg, hi, lo, jnp.zeros((n_blocks, bw - 2, 2 * bw), _BF16)], axis=1)


def _layer(x, c, ctx, c_ctx, lp, g_final, *, tile, ctx_tile):
    b, t, d = x.shape
    dc = lp["w_dw"].shape[-1]
    wl = lp["w_lru_conv"].shape[-1]
    col_lru = 2 * dc

    cc = jnp.concatenate([c, c_ctx[None, :]], axis=0)
    cc = jnp.pad(cc, ((0, SUBLANES - cc.shape[0] % SUBLANES), (0, 0)))
    m_all = _modulation(cc, lp["w_mod"], lp["b_mod"])
    m3 = m_all[:b, None, :]
    mc3 = jnp.broadcast_to(m_all[b][None, None, :], (b, 1, N_MOD * d))

    wu1, wd1 = lp["w_ffn1_up"], lp["w_ffn1_down"]
    wu2, wd2 = lp["w_ffn2_up"], lp["w_ffn2_down"]
    w_in = lp["w_in"].astype(_BF16)
    lam = lp["lru_lambda"]
    gates = [_gate_weights(lp["w_rec_gate"], lp["b_rec_gate"], lp["w_in_gate"], lp["b_in_gate"], k)
             for k in range(2)]
    cw, cb = lp["w_lru_conv"], lp["b_lru_conv"].reshape(1, wl)

    def lru_pair(ux, h0f, h0b, tl, hf_gl=None, groups=1):
        hf, hf_last, xr = _lru(ux, cw, cb, gates[0], lam[0:1], h0f, tile=tl, groups=groups,
                               reverse=False, emit_xr=True)
        extra = {} if hf_gl is None else dict(hf=hf, gl=hf_gl)
        out, hb_last = _lru(None, None, None, gates[1], lam[1:2], h0b, tile=tl, groups=groups,
                            reverse=True, xr=xr, **extra)
        return out, hf_last[:, 0:1, :], hb_last[:, 0:1, :]

    xc1 = _ffn(ctx, mc3, lp["g_n1"], wu1, wd1, tile=ctx_tile, il_in=False, il_out=True,
               mod_idx=(0, 1, 2))
    (uxc,) = _inproj(xc1, mc3, lp["g_n2"], w_in[:, col_lru:col_lru + wl],
                     lp["b_in"][col_lru:col_lru + wl], tm=ctx_tile, dc=dc, wl=wl,
                     lru_only=True, mod_idx=(3, 4))
    zeros = jnp.zeros((b, 1, wl), _F32)
    _, h0f, h0b = lru_pair(uxc, zeros, zeros, ctx_tile)

    pos = _grid_pos_tables(t, d)
    x1 = _ffn(x, m3, lp["g_n1"], wu1, wd1, tile=tile, il_in=False, il_out=True,
              mod_idx=(0, 1, 2), pos=pos)
    groups = 2 if (t // tile) % 2 == 0 else 1
    u, ux, gl, gts = _inproj(x1, m3, lp["g_n2"], w_in, lp["b_in"], tm=groups * tile, dc=dc, wl=wl,
                             lru_only=False, mod_idx=(3, 4))
    yl, _, _ = lru_pair(ux, h0f, h0b, tile, hf_gl=gl, groups=groups)
    x2 = _mix(u, yl, gts, x1, m3, lp["w_dw"], lp["b_dw"], lp["g_ln"], lp["b_ln"],
              lp["w_conf_out"].astype(_BF16), lp["w_lru_out"].astype(_BF16),
              lp["w_out"].astype(_BF16), tile=tile, gate_idx=5)
    return _ffn(x2, m3, lp["g_n3"], wu2, wd2, tile=tile, il_in=True, il_out=False,
                mod_idx=(6, 7, 8), g_final=g_final)


def _forward(x, c, ctx, c_ctx, params, g_final, *, tile=512, ctx_tile=256):
    depth = params["w_mod"].shape[0]
    assert depth == 1, "only the single-layer (context read-only) block is implemented"
    lp = {k: v[0] for k, v in params.items()}
    return _layer(x, c, ctx, c_ctx, lp, g_final, tile=tile, ctx_tile=ctx_tile)


def kernel(x, c, ctx, c_ctx, w_mod, b_mod, g_n1, w_ffn1_up, w_ffn1_down, g_n2, w_in, b_in, w_dw, b_dw, g_ln, b_ln, w_conf_out, w_lru_conv, b_lru_conv, w_rec_gate, b_rec_gate, w_in_gate, b_in_gate, lru_lambda, w_lru_out, w_out, g_n3, w_ffn2_up, w_ffn2_down, g_final):
    params = dict(w_mod=w_mod, b_mod=b_mod, g_n1=g_n1, w_ffn1_up=w_ffn1_up, w_ffn1_down=w_ffn1_down,
                  g_n2=g_n2, w_in=w_in, b_in=b_in, w_dw=w_dw, b_dw=b_dw, g_ln=g_ln, b_ln=b_ln,
                  w_conf_out=w_conf_out, w_lru_conv=w_lru_conv, b_lru_conv=b_lru_conv,
                  w_rec_gate=w_rec_gate, b_rec_gate=b_rec_gate, w_in_gate=w_in_gate,
                  b_in_gate=b_in_gate, lru_lambda=lru_lambda, w_lru_out=w_lru_out, w_out=w_out,
                  g_n3=g_n3, w_ffn2_up=w_ffn2_up, w_ffn2_down=w_ffn2_down)
    return _forward(x, c, ctx, c_ctx, params, g_final)
```
